```python
import jax
import jax.numpy as jnp
from jax import lax
import numpy as np

D_MODEL = 1024
BATCH = 16
SEQ = 2048
DEPTH = 4

N_MIXERS = 2
N_A_LAYERS = (DEPTH + 1) // 2
N_B_LAYERS = DEPTH // 2
HGRN_EXPAND = 128
HGRN_HEADS = D_MODEL // HGRN_EXPAND
HGRN_HEAD_V = D_MODEL // HGRN_HEADS
HGRN_CHUNK = 16
HGRN_IN = 4 * D_MODEL
RET_HEADS = 4
RET_QK_DIM = D_MODEL // RET_HEADS
RET_V_DIM = 2 * D_MODEL // RET_HEADS
RET_QK = RET_HEADS * RET_QK_DIM
RET_VD = RET_HEADS * RET_V_DIM
RET_IN = 2 * RET_QK + 2 * RET_VD
RET_CHUNK = 128
ROPE_BASE = 10000.0
N_EXPERTS = 32
TOP_K = 4
D_FF = D_MODEL
SWIGLU_LIMIT = 7.0
SWIGLU_ALPHA = 1.702
EXPERT_BLOCK = 256
DN_ALPHA = (2 * DEPTH) ** 0.25
DN_BETA = (8 * DEPTH) ** -0.25
LN_EPS = 1e-5
RMS_EPS = 1e-6

kernel_name = "hgrn2_retention_moe_deepnorm"


def layer_norm(x, g, b):
    xf = x.astype(jnp.float32)
    mu = jnp.mean(xf, axis=-1, keepdims=True)
    var = jnp.mean(jnp.square(xf - mu), axis=-1, keepdims=True)
    return ((xf - mu) * lax.rsqrt(var + LN_EPS) * g.astype(jnp.float32) + b.astype(jnp.float32)).astype(x.dtype)


def rms_norm(x, g):
    xf = x.astype(jnp.float32)
    return xf * lax.rsqrt(jnp.mean(jnp.square(xf), axis=-1, keepdims=True) + RMS_EPS) * g.astype(jnp.float32)


def rotate_every_two(t):
    t1 = t[..., ::2]
    t2 = t[..., 1::2]
    return jnp.stack((-t2, t1), axis=-1).reshape(t.shape)


def rotary_tables(seq_len):
    pos = jnp.arange(seq_len, dtype=jnp.float32)
    inv = 1.0 / (ROPE_BASE ** jnp.linspace(0.0, 1.0, RET_QK_DIM // 2, dtype=jnp.float32))
    theta = pos[:, None] * jnp.repeat(inv, 2)[None, :]
    return jnp.cos(theta), jnp.sin(theta)


def chunk_gated_linear_recurrence(q, k, v, log_f, chunk):
    B, H, S, K = q.shape
    V = v.shape[-1]
    n = S // chunk
    q = q.reshape(B, H, n, chunk, K)
    k = k.reshape(B, H, n, chunk, K)
    v = v.reshape(B, H, n, chunk, V)
    b = jnp.cumsum(log_f.reshape(B, H, n, chunk, K), axis=3)
    b_last = b[:, :, :, -1:, :]
    q_dec = q * jnp.exp(b)
    k_inv = k * jnp.exp(-b)
    k_st = k * jnp.exp(b_last - b)
    causal = jnp.tril(jnp.ones((chunk, chunk), dtype=bool))
    scores = jnp.where(causal, jnp.einsum('bhnck,bhnsk->bhncs', q_dec, k_inv), 0.0)
    o_intra = jnp.einsum('bhncs,bhnsv->bhncv', scores, v)
    chunk_decay = jnp.exp(b_last[:, :, :, 0, :])

    def step(state, inp):
        qc, kc, vc, dc = inp
        out = jnp.einsum('bhck,bhkv->bhcv', qc, state)
        state = state * dc[..., None] + jnp.einsum('bhck,bhcv->bhkv', kc, vc)
        return state, out

    init = jnp.zeros((B, H, K, V), jnp.float32)
    xs = (jnp.moveaxis(q_dec, 2, 0), jnp.moveaxis(k_st, 2, 0), jnp.moveaxis(v, 2, 0), jnp.moveaxis(chunk_decay, 2, 0))
    _, o_inter = lax.scan(step, init, xs)
    return (o_intra + jnp.moveaxis(o_inter, 0, 2)).reshape(B, H, S, V)


def chunk_retention(q, k, v, log_gamma, chunk):
    B, H, S, K = q.shape
    V = v.shape[-1]
    n = S // chunk
    q = q.reshape(B, H, n, chunk, K)
    k = k.reshape(B, H, n, chunk, K)
    v = v.reshape(B, H, n, chunk, V)
    idx = jnp.arange(chunk, dtype=jnp.float32)
    lg = log_gamma[:, None]
    rel = idx[:, None] - idx[None, :]
    decay = jnp.where(rel >= 0, jnp.exp(rel[None] * log_gamma[:, None, None]), 0.0)
    scores = jnp.einsum('bhnck,bhnsk->bhncs', q, k) * decay[None, :, None]
    o_intra = jnp.einsum('bhncs,bhnsv->bhncv', scores, v)
    q_in = q * jnp.exp((idx + 1.0)[None, :] * lg)[None, :, None, :, None]
    k_st = k * jnp.exp((chunk - 1.0 - idx)[None, :] * lg)[None, :, None, :, None]
    chunk_decay = jnp.exp(chunk * log_gamma)[None, :, None, None]

    def step(state, inp):
        qc, kc, vc = inp
        out = jnp.einsum('bhck,bhkv->bhcv', qc, state)
        state = state * chunk_decay + jnp.einsum('bhck,bhcv->bhkv', kc, vc)
        return state, out

    init = jnp.zeros((B, H, K, V), jnp.float32)
    xs = (jnp.moveaxis(q_in, 2, 0), jnp.moveaxis(k_st, 2, 0), jnp.moveaxis(v, 2, 0))
    _, o_inter = lax.scan(step, init, xs)
    return (o_intra + jnp.moveaxis(o_inter, 0, 2)).reshape(B, H, S, V)


def hgrn2_mixer(x, w_in, norm_g, w_out, lb):
    B, S, D = x.shape
    q, f, i, g = jnp.split(x @ w_in, 4, axis=-1)
    sf = f.astype(jnp.float32)
    forget = lb + (1.0 - lb) * jax.nn.sigmoid(sf)
    k = (1.0 - lb) * jax.nn.sigmoid(-sf)
    log_f = jnp.log(forget)

    def heads(t, d):
        return t.reshape(B, S, HGRN_HEADS, d).transpose(0, 2, 1, 3).astype(jnp.float32)

    o = chunk_gated_linear_recurrence(heads(jax.nn.silu(q), HGRN_EXPAND), heads(k, HGRN_EXPAND),
                                      heads(i, HGRN_HEAD_V), heads(log_f, HGRN_EXPAND), HGRN_CHUNK)
    o = o.transpose(0, 2, 1, 3).reshape(B, S, D)
    o = rms_norm(o, norm_g) * jax.nn.silu(g.astype(jnp.float32))
    return o.astype(x.dtype) @ w_out


def retention_mixer(x, w_in, w_out, cos, sin):
    B, S, _ = x.shape
    h = x @ w_in
    q = h[..., :RET_QK].reshape(B, S, RET_HEADS, RET_QK_DIM)
    k = h[..., RET_QK:2 * RET_QK].reshape(B, S, RET_HEADS, RET_QK_DIM)
    v = h[..., 2 * RET_QK:2 * RET_QK + RET_VD].reshape(B, S, RET_HEADS, RET_V_DIM)
    g = h[..., 2 * RET_QK + RET_VD:]
    c = cos[None, :, None, :]
    s = sin[None, :, None, :]
    q = q * c + rotate_every_two(q) * s
    k = (k * c + rotate_every_two(k) * s) * (RET_QK_DIM ** -0.5)

    def bhsd(t):
        return t.transpose(0, 2, 1, 3).astype(jnp.float32)

    log_gamma = jnp.log(1.0 - 2.0 ** (-5.0 - jnp.arange(RET_HEADS, dtype=jnp.float32)))
    o = chunk_retention(bhsd(q), bhsd(k), bhsd(v), log_gamma, RET_CHUNK)
    mu = jnp.mean(o, axis=-1, keepdims=True)
    var = jnp.mean(jnp.square(o - mu), axis=-1, keepdims=True)
    o = (o - mu) * lax.rsqrt(var + LN_EPS)
    o = o.transpose(0, 2, 1, 3).reshape(B, S, RET_VD)
    return (jax.nn.silu(g.astype(jnp.float32)) * o).astype(x.dtype) @ w_out


def moe_ffn(x2, w_r, b_r, w_gu, b_gu, w_dn, b_dn):
    N, D = x2.shape
    logits = (x2 @ w_r).astype(jnp.float32) + b_r.astype(jnp.float32)
    top_val, top_idx = lax.top_k(logits, TOP_K)
    gate = jax.nn.softmax(top_val, axis=-1)
    A = N * TOP_K
    flat_e = top_idx.reshape(-1)
    flat_tok = jnp.arange(A, dtype=jnp.int32) // TOP_K
    flat_w = gate.reshape(-1)
    order = jnp.argsort(flat_e)
    se = flat_e[order]
    counts = jnp.bincount(flat_e, length=N_EXPERTS).astype(jnp.int32)
    padded = (counts + EXPERT_BLOCK - 1) // EXPERT_BLOCK * EXPERT_BLOCK
    start = jnp.cumsum(counts) - counts
    pend = jnp.cumsum(padded)
    pstart = pend - padded
    dest = pstart[se] + (jnp.arange(A, dtype=jnp.int32) - start[se])
    n_blocks = -(-A // EXPERT_BLOCK) + N_EXPERTS
    n_rows = n_blocks * EXPERT_BLOCK
    row_tok = jnp.zeros((n_rows,), jnp.int32).at[dest].set(flat_tok[order])
    row_w = jnp.zeros((n_rows,), x2.dtype).at[dest].set(flat_w[order].astype(x2.dtype))
    block_starts = jnp.arange(n_blocks, dtype=jnp.int32) * EXPERT_BLOCK
    blk_e = jnp.minimum(jnp.searchsorted(pend, block_starts, side='right'), N_EXPERTS - 1)

    def expert_block(args):
        tok, wt, e = args
        h = x2[tok] @ w_gu[e] + b_gu[e]
        glu = jnp.minimum(h[:, ::2], SWIGLU_LIMIT)
        lin = jnp.clip(h[:, 1::2], -SWIGLU_LIMIT, SWIGLU_LIMIT)
        a = glu * jax.nn.sigmoid(SWIGLU_ALPHA * glu) * (lin + 1.0)
        return (a @ w_dn[e] + b_dn[e]) * wt[:, None]

    yb = lax.map(expert_block, (row_tok.reshape(n_blocks, EXPERT_BLOCK),
                                row_w.reshape(n_blocks, EXPERT_BLOCK), blk_e))
    return jax.ops.segment_sum(yb.reshape(n_rows, D), row_tok, num_segments=N)


def setup_inputs(seed: int = 0) -> dict:
    key = jax.random.key(seed)
    ks = jax.random.split(key, 20)

    def nrm(k, shape, scale):
        return jax.random.normal(k, shape, jnp.float32) * scale

    s_in = D_MODEL ** -0.5
    hgrn_cols = jnp.concatenate([jnp.ones((2 * D_MODEL,)), jnp.full((D_MODEL,), DN_BETA),
                                 jnp.ones((D_MODEL,))]).astype(jnp.float32)
    ret_cols = jnp.concatenate([jnp.ones((2 * RET_QK,)), jnp.full((RET_VD,), DN_BETA),
                                jnp.ones((RET_VD,))]).astype(jnp.float32)
    return {
        "x": nrm(ks[0], (BATCH, SEQ, D_MODEL), 1.0),
        "hgrn_w_in": nrm(ks[1], (N_A_LAYERS, D_MODEL, HGRN_IN), s_in) * hgrn_cols,
        "hgrn_norm_g": 1.0 + nrm(ks[2], (N_A_LAYERS, D_MODEL), 0.02),
        "hgrn_w_out": nrm(ks[3], (N_A_LAYERS, D_MODEL, D_MODEL), s_in * DN_BETA),
        "lower_bounds": nrm(ks[4], (DEPTH, D_MODEL), 0.1),
        "ret_w_in": nrm(ks[5], (N_B_LAYERS, D_MODEL, RET_IN), s_in) * ret_cols,
        "ret_w_out": nrm(ks[6], (N_B_LAYERS, RET_VD, D_MODEL), RET_VD ** -0.5 * DN_BETA),
        "ln_mix_g": 1.0 + nrm(ks[7], (DEPTH, D_MODEL), 0.02),
        "ln_mix_b": nrm(ks[8], (DEPTH, D_MODEL), 0.02),
        "ln_ffn_g": 1.0 + nrm(ks[9], (DEPTH, D_MODEL), 0.02),
        "ln_ffn_b": nrm(ks[10], (DEPTH, D_MODEL), 0.02),
        "router_w": nrm(ks[11], (DEPTH, D_MODEL, N_EXPERTS), s_in),
        "router_b": nrm(ks[12], (DEPTH, N_EXPERTS), 0.01),
        "expert_w_gu": nrm(ks[13], (DEPTH, N_EXPERTS, D_MODEL, 2 * D_FF), s_in),
        "expert_b_gu": nrm(ks[14], (DEPTH, N_EXPERTS, 2 * D_FF), 0.02),
        "expert_w_down": nrm(ks[15], (DEPTH, N_EXPERTS, D_FF, D_MODEL), D_FF ** -0.5 * DN_BETA),
        "expert_b_down": nrm(ks[16], (DEPTH, N_EXPERTS, D_MODEL), 0.02),
    }


def reference(x, hgrn_w_in, hgrn_norm_g, hgrn_w_out, lower_bounds, ret_w_in, ret_w_out,
              ln_mix_g, ln_mix_b, ln_ffn_g, ln_ffn_b, router_w, router_b,
              expert_w_gu, expert_b_gu, expert_w_down, expert_b_down):
    B, S, D = x.shape
    lb_soft = jax.nn.softmax(lower_bounds.astype(jnp.float32), axis=0)
    lbs = jnp.cumsum(lb_soft, axis=0) - lb_soft[0]
    cos, sin = rotary_tables(S)
    for i in range(DEPTH):
        j = i // N_MIXERS
        if i % N_MIXERS == 0:
            mix = hgrn2_mixer(x, hgrn_w_in[j], hgrn_norm_g[j], hgrn_w_out[j], lbs[i])
        else:
            mix = retention_mixer(x, ret_w_in[j], ret_w_out[j], cos, sin)
        x = layer_norm(DN_ALPHA * x + mix, ln_mix_g[i], ln_mix_b[i])
        ffn = moe_ffn(x.reshape(B * S, D), router_w[i], router_b[i], expert_w_gu[i],
                      expert_b_gu[i], expert_w_down[i], expert_b_down[i]).reshape(B, S, D)
        x = layer_norm(DN_ALPHA * x + ffn.astype(x.dtype), ln_ffn_g[i], ln_ffn_b[i])
    return x
```

```python
import functools
import math

import jax
import jax.numpy as jnp
from jax import lax
from jax.experimental import pallas as pl
from jax.experimental.pallas import tpu as pltpu

HGRN_HEAD = 128
RET_HEADS = 4
ROPE_BASE = 10000.0
N_EXPERTS = 32
TOP_K = 4
SWIGLU_LIMIT = 7.0
SWIGLU_ALPHA = 1.702
LN_EPS = 1e-5
RMS_EPS = 1e-6

V7X_VMEM_LIMIT = 56 * 1024 * 1024
SEQ_TILE = 256
HGRN_CHUNK = 64
HGRN_SUB = 16
RET_CHUNK = 128
ROUTER_TILE = 512
EXPERT_ROWS = 256
MOVE_TILE = 256

F32 = jnp.float32
BF16 = jnp.bfloat16

_NT = (((1,), (1,)), ((), ()))
_TN = (((0,), (0,)), ((), ()))


def _dot(a, b, dims=None):
    if dims is None:
        return jnp.dot(a, b, preferred_element_type=F32)
    return lax.dot_general(a, b, dims, preferred_element_type=F32)


def _layer_norm(v, g, b):
    mu = jnp.mean(v, axis=-1, keepdims=True)
    d = v - mu
    var = jnp.mean(d * d, axis=-1, keepdims=True)
    return d * lax.rsqrt(var + LN_EPS) * g + b


def _split3(v):
    hi = v.astype(BF16)
    r1 = v - hi.astype(F32)
    mid = r1.astype(BF16)
    lo = (r1 - mid.astype(F32)).astype(BF16)
    return hi, mid, lo


def _hgrn_kernel(x_ref, w_in_ref, lb_ref, ng_ref, w_out_ref, lg_ref, lbias_ref,
                 o_ref, h_ref, og_ref, st_ref, *, dn_alpha):
    ts, d = x_ref.shape[1], x_ref.shape[2]
    n_heads = d // HGRN_HEAD
    c = HGRN_CHUNK
    n_sub = c // HGRN_SUB

    @pl.when(pl.program_id(1) == 0)
    def _():
        st_ref[...] = jnp.zeros_like(st_ref)

    h_ref[...] = _dot(x_ref[0].astype(BF16), w_in_ref[...])

    lb = lb_ref[...]
    one_m_lb = 1.0 - lb
    row = lax.broadcasted_iota(jnp.int32, (c, c), 0)
    col = lax.broadcasted_iota(jnp.int32, (c, c), 1)
    tril = jnp.where(row >= col, 1.0, 0.0).astype(BF16)

    def chunk_body(ci, carry):
        r0 = pl.multiple_of(ci * c, c)
        rows = pl.ds(r0, c)
        q = h_ref[rows, 0:d]
        sf = h_ref[rows, d:2 * d]
        v = h_ref[rows, 2 * d:3 * d]
        g = h_ref[rows, 3 * d:4 * d]
        qa = q * jax.nn.sigmoid(q)
        forget = lb + one_m_lb * jax.nn.sigmoid(sf)
        k = one_m_lb * jax.nn.sigmoid(-sf)
        logf = jnp.log(forget)
        hi, mid, lo = _split3(logf)
        gc = _dot(tril, lo) + _dot(tril, mid) + _dot(tril, hi)
        g_last = gc[c - 1:c, :]
        q_in = (qa * jnp.exp(gc)).astype(BF16)
        k_out = (k * jnp.exp(g_last - gc)).astype(BF16)
        dec = jnp.exp(g_last)
        vb = v.astype(BF16)
        outs = []
        for hd in range(n_heads):
            sl = slice(HGRN_HEAD * hd, HGRN_HEAD * (hd + 1))
            st = st_ref[hd]
            o_h = _dot(q_in[:, sl], st.astype(BF16), _NT)
            g_h = gc[:, sl]
            qa_h = qa[:, sl]
            k_h = k[:, sl]
            for i in range(n_sub):
                s0 = HGRN_SUB * i
                g_end = g_h[s0 + HGRN_SUB - 1:s0 + HGRN_SUB, :]
                qm = (qa_h[s0:, :] * jnp.exp(g_h[s0:, :] - g_end)).astype(BF16)
                ks = (k_h[s0:s0 + HGRN_SUB, :] * jnp.exp(g_end - g_h[s0:s0 + HGRN_SUB, :])).astype(BF16)
                sc = _dot(qm, ks, _NT)
                rr = lax.broadcasted_iota(jnp.int32, sc.shape, 0)
                cc = lax.broadcasted_iota(jnp.int32, sc.shape, 1)
                sc = jnp.where(rr >= cc, sc, 0.0).astype(BF16)
                part = _dot(sc, vb[s0:s0 + HGRN_SUB, sl])
                if s0:
                    part = jnp.concatenate([jnp.zeros((s0, HGRN_HEAD), F32), part], axis=0)
                o_h = o_h + part
            upd = _dot(vb[:, sl], k_out[:, sl], _TN)
            st_ref[hd] = st * dec[:, sl] + upd
            outs.append(o_h)
        o = jnp.concatenate(outs, axis=1)
        o = o * lax.rsqrt(jnp.mean(o * o, axis=-1, keepdims=True) + RMS_EPS) * ng_ref[...]
        og_ref[rows, :] = (o * (g * jax.nn.sigmoid(g))).astype(BF16)
        return carry

    lax.fori_loop(0, ts // c, chunk_body, 0)

    mix = _dot(og_ref[...], w_out_ref[...])
    o_ref[0] = _layer_norm(dn_alpha * x_ref[0] + mix, lg_ref[...], lbias_ref[...])


def _hgrn_layer(x, w_in, lb, norm_g, w_out, ln_g, ln_b, dn_alpha):
    b, s, d = x.shape
    ts = min(SEQ_TILE, s)
    vec = pl.BlockSpec((1, d), lambda bi, si: (0, 0))
    return pl.pallas_call(
        functools.partial(_hgrn_kernel, dn_alpha=dn_alpha),
        grid=(b, s // ts),
        in_specs=[
            pl.BlockSpec((1, ts, d), lambda bi, si: (bi, si, 0)),
            pl.BlockSpec((d, 4 * d), lambda bi, si: (0, 0)),
            vec, vec,
            pl.BlockSpec((d, d), lambda bi, si: (0, 0)),
            vec, vec,
        ],
        out_specs=pl.BlockSpec((1, ts, d), lambda bi, si: (bi, si, 0)),
        out_shape=jax.ShapeDtypeStruct((b, s, d), F32),
        scratch_shapes=[
            pltpu.VMEM((ts, 4 * d), F32),
            pltpu.VMEM((ts, d), BF16),
            pltpu.VMEM((d // HGRN_HEAD, HGRN_HEAD, HGRN_HEAD), F32),
        ],
        compiler_params=pltpu.CompilerParams(
            dimension_semantics=("arbitrary", "arbitrary"),
            vmem_limit_bytes=V7X_VMEM_LIMIT),
        name="hgrn_mixer",
    )(x, w_in, lb, norm_g, w_out, ln_g, ln_b)


def _ret_kernel(x_ref, w_in_ref, cos_ref, sin_e_ref, sin_o_ref, w_out_ref, lg_ref, lbias_ref,
                o_ref, h_ref, og_ref, st_ref, *, dn_alpha):
    ts, d = x_ref.shape[1], x_ref.shape[2]
    qk = d
    vd = 2 * d
    hk = qk // RET_HEADS
    hv = vd // RET_HEADS
    c = RET_CHUNK

    @pl.when(pl.program_id(1) == 0)
    def _():
        st_ref[...] = jnp.zeros_like(st_ref)

    h_ref[...] = _dot(x_ref[0].astype(BF16), w_in_ref[...])

    ri = lax.broadcasted_iota(jnp.int32, (c, c), 0)
    cj = lax.broadcasted_iota(jnp.int32, (c, c), 1)
    rel = (ri - cj).astype(F32)
    idx_col = lax.broadcasted_iota(jnp.int32, (c, 1), 0).astype(F32)

    def chunk_body(ci, carry):
        r0 = pl.multiple_of(ci * c, c)
        rows = pl.ds(r0, c)
        cos = jnp.tile(cos_ref[rows, :], (1, RET_HEADS))
        sin_e = jnp.tile(sin_e_ref[rows, :], (1, RET_HEADS))
        sin_o = jnp.tile(sin_o_ref[rows, :], (1, RET_HEADS))

        def rope(t):
            nxt = pltpu.roll(t, qk - 1, 1)
            prv = pltpu.roll(t, 1, 1)
            return t * cos + nxt * sin_e + prv * sin_o

        q = rope(h_ref[rows, 0:qk]).astype(BF16)
        k = rope(h_ref[rows, qk:2 * qk]) * (hk ** -0.5)
        v = h_ref[rows, 2 * qk:2 * qk + vd].astype(BF16)
        g = h_ref[rows, 2 * qk + vd:2 * qk + 2 * vd]
        outs = []
        for hd in range(RET_HEADS):
            lgam = math.log(1.0 - 2.0 ** (-5.0 - hd))
            decay = jnp.where(rel >= 0, jnp.exp(rel * lgam), 0.0)
            xi = jnp.exp((idx_col + 1.0) * lgam)
            zeta = jnp.exp((c - 1.0 - idx_col) * lgam)
            q_h = q[:, hk * hd:hk * (hd + 1)]
            k_h = k[:, hk * hd:hk * (hd + 1)]
            v_h = v[:, hv * hd:hv * (hd + 1)]
            st = st_ref[hd]
            sc = (_dot(q_h, k_h.astype(BF16), _NT) * decay).astype(BF16)
            o_h = _dot(sc, v_h) + _dot(q_h, st.astype(BF16)) * xi
            upd = _dot((k_h * zeta).astype(BF16), v_h, _TN)
            st_ref[hd] = st * math.exp(c * lgam) + upd
            mu = jnp.mean(o_h, axis=-1, keepdims=True)
            dlt = o_h - mu
            var = jnp.mean(dlt * dlt, axis=-1, keepdims=True)
            outs.append(dlt * lax.rsqrt(var + LN_EPS))
        o = jnp.concatenate(outs, axis=1)
        og_ref[rows, :] = ((g * jax.nn.sigmoid(g)) * o).astype(BF16)
        return carry

    lax.fori_loop(0, ts // c, chunk_body, 0)

    mix = _dot(og_ref[...], w_out_ref[...])
    o_ref[0] = _layer_norm(dn_alpha * x_ref[0] + mix, lg_ref[...], lbias_ref[...])


def _ret_layer(x, w_in, cos, sin_e, sin_o, w_out, ln_g, ln_b, dn_alpha):
    b, s, d = x.shape
    ts = min(SEQ_TILE, s)
    n_in = w_in.shape[1]
    hk = d // RET_HEADS
    vec = pl.BlockSpec((1, d), lambda bi, si: (0, 0))
    tab = pl.BlockSpec((ts, hk), lambda bi, si: (si, 0))
    return pl.pallas_call(
        functools.partial(_ret_kernel, dn_alpha=dn_alpha),
        grid=(b, s // ts),
        in_specs=[
            pl.BlockSpec((1, ts, d), lambda bi, si: (bi, si, 0)),
            pl.BlockSpec((d, n_in), lambda bi, si: (0, 0)),
            tab, tab, tab,
            pl.BlockSpec((2 * d, d), lambda bi, si: (0, 0)),
            vec, vec,
        ],
        out_specs=pl.BlockSpec((1, ts, d), lambda bi, si: (bi, si, 0)),
        out_shape=jax.ShapeDtypeStruct((b, s, d), F32),
        scratch_shapes=[
            pltpu.VMEM((ts, n_in), F32),
            pltpu.VMEM((ts, 2 * d), BF16),
            pltpu.VMEM((RET_HEADS, hk, 2 * d // RET_HEADS), F32),
        ],
        compiler_params=pltpu.CompilerParams(
            dimension_semantics=("arbitrary", "arbitrary"),
            vmem_limit_bytes=V7X_VMEM_LIMIT),
        name="ret_mixer",
    )(x, w_in, cos, sin_e, sin_o, w_out, ln_g, ln_b)


def _rotary_tables(seq_len, hk):
    pos = jnp.arange(seq_len, dtype=F32)
    inv = 1.0 / (ROPE_BASE ** jnp.linspace(0.0, 1.0, hk // 2, dtype=F32))
    theta = pos[:, None] * jnp.repeat(inv, 2)[None, :]
    cos, sin = jnp.cos(theta), jnp.sin(theta)
    even = (jnp.arange(hk) % 2 == 0)[None, :]
    return cos, jnp.where(even, -sin, 0.0), jnp.where(even, 0.0, sin)


def _router_kernel(x_ref, w_ref, b_ref, idx_ref, gate_ref, rank_ref, cnt_ref, acc_ref):
    t = x_ref.shape[0]
    e = w_ref.shape[0]

    @pl.when(pl.program_id(0) == 0)
    def _():
        acc_ref[...] = jnp.zeros_like(acc_ref)

    xh, xm, xl = _split3(x_ref[...])
    wh, wm, wl = _split3(w_ref[...])
    logits = (_dot(wl, xh, _NT) + _dot(wh, xl, _NT) + _dot(wm, xm, _NT)
              + _dot(wm, xh, _NT) + _dot(wh, xm, _NT) + _dot(wh, xh, _NT)) + b_ref[...]

    eid = lax.broadcasted_iota(jnp.int32, (e, t), 0)
    ti = lax.broadcasted_iota(jnp.int32, (t, t), 0)
    tj = lax.broadcasted_iota(jnp.int32, (t, t), 1)
    before = jnp.where(ti < tj, 1.0, 0.0).astype(BF16)
    work = logits
    vals, idxs, ranks = [], [], []
    base = acc_ref[...]
    for _ in range(TOP_K):
        m = jnp.max(work, axis=0, keepdims=True)
        sel = jnp.min(jnp.where(work == m, eid, e), axis=0, keepdims=True)
        hit = eid == sel
        work = jnp.where(hit, -jnp.inf, work)
        onehot = jnp.where(hit, 1.0, 0.0)
        excl = _dot(onehot.astype(BF16), before)
        rank = jnp.sum(onehot * (excl + base), axis=0, keepdims=True)
        base = base + jnp.sum(onehot, axis=1, keepdims=True)
        vals.append(m)
        idxs.append(sel)
        ranks.append(rank)
    acc_ref[...] = base
    cnt_ref[...] = base.astype(jnp.int32)
    ex = [jnp.exp(vv - vals[0]) for vv in vals]
    den = ex[0] + ex[1] + ex[2] + ex[3]
    gate_ref[...] = jnp.concatenate(ex, axis=0) / den
    idx_ref[...] = jnp.concatenate(idxs, axis=0)
    rank_ref[...] = jnp.concatenate(ranks, axis=0).astype(jnp.int32)


def _router(x2, w_t, b_col):
    n, d = x2.shape
    e = w_t.shape[0]
    t = min(ROUTER_TILE, n)
    kt = pl.BlockSpec((TOP_K, t), lambda i: (0, i))
    return pl.pallas_call(
        _router_kernel,
        grid=(n // t,),
        in_specs=[
            pl.BlockSpec((t, d), lambda i: (i, 0)),
            pl.BlockSpec((e, d), lambda i: (0, 0)),
            pl.BlockSpec((e, 1), lambda i: (0, 0)),
        ],
        out_specs=[kt, kt, kt, pl.BlockSpec((e, 1), lambda i: (0, 0))],
        out_shape=[
            jax.ShapeDtypeStruct((TOP_K, n), jnp.int32),
            jax.ShapeDtypeStruct((TOP_K, n), F32),
            jax.ShapeDtypeStruct((TOP_K, n), jnp.int32),
            jax.ShapeDtypeStruct((e, 1), jnp.int32),
        ],
        scratch_shapes=[pltpu.VMEM((e, 1), F32)],
        compiler_params=pltpu.CompilerParams(dimension_semantics=("arbitrary",)),
        name="moe_router",
    )(x2, w_t, b_col)


def _row_copy(src_ref, src_row, dst_ref, dst_row, sem):
    return pltpu.make_async_copy(src_ref.at[pl.ds(src_row, 1)], dst_ref.at[pl.ds(dst_row, 1)], sem)


def _dispatch_kernel(dest_hbm, x_ref, xs_in, xs_out, dest_smem, sem_idx, sem_rows):
    del xs_in
    t = x_ref.shape[0]
    i = pl.program_id(0)
    idx_copy = pltpu.make_async_copy(dest_hbm.at[i], dest_smem, sem_idx)
    idx_copy.start()
    idx_copy.wait()

    def issue(tt, carry):
        for kk in range(TOP_K):
            _row_copy(x_ref, tt, xs_out, dest_smem[kk * t + tt], sem_rows).start()
        return carry

    lax.fori_loop(0, t, issue, 0, unroll=8)

    def drain(tt, carry):
        for kk in range(TOP_K):
            _row_copy(x_ref, 0, xs_out, 0, sem_rows).wait()
        return carry

    lax.fori_loop(0, t, drain, 0, unroll=8)


def _dispatch(x2, dest_blocks, n_rows):
    n, d = x2.shape
    t = dest_blocks.shape[1] // TOP_K
    xs0 = jnp.zeros((n_rows, d), x2.dtype)
    return pl.pallas_call(
        _dispatch_kernel,
        grid=(n // t,),
        in_specs=[
            pl.BlockSpec(memory_space=pl.ANY),
            pl.BlockSpec((t, d), lambda i: (i, 0)),
            pl.BlockSpec(memory_space=pl.ANY),
        ],
        out_specs=pl.BlockSpec(memory_space=pl.ANY),
        out_shape=jax.ShapeDtypeStruct((n_rows, d), x2.dtype),
        scratch_shapes=[
            pltpu.SMEM((TOP_K * t,), jnp.int32),
            pltpu.SemaphoreType.DMA,
            pltpu.SemaphoreType.DMA,
        ],
        input_output_aliases={2: 0},
        compiler_params=pltpu.CompilerParams(dimension_semantics=("arbitrary",)),
        name="moe_dispatch",
    )(dest_blocks, x2, xs0)


def _expert_kernel(blk_e_ref, xs_ref, wg_ref, wl_ref, bg_ref, bl_ref, wd_ref, bd_ref, y_ref):
    del blk_e_ref
    xb = xs_ref[...].astype(BF16)
    hg = _dot(xb, wg_ref[0]) + bg_ref[0]
    hl = _dot(xb, wl_ref[0]) + bl_ref[0]
    glu = jnp.minimum(hg, SWIGLU_LIMIT)
    lin = jnp.clip(hl, -SWIGLU_LIMIT, SWIGLU_LIMIT)
    a = glu * jax.nn.sigmoid(SWIGLU_ALPHA * glu) * (lin + 1.0)
    y_ref[...] = _dot(a.astype(BF16), wd_ref[0]) + bd_ref[0]


def _experts(xs, blk_e, w_glu, w_lin, b_glu, b_lin, w_dn, b_dn):
    n_rows, d = xs.shape
    ff = w_glu.shape[2]
    bm = EXPERT_ROWS
    wspec = lambda r, c: pl.BlockSpec((1, r, c), lambda i, be: (be[i], 0, 0))
    grid_spec = pltpu.PrefetchScalarGridSpec(
        num_scalar_prefetch=1,
        grid=(n_rows // bm,),
        in_specs=[
            pl.BlockSpec((bm, d), lambda i, be: (i, 0)),
            wspec(d, ff), wspec(d, ff), wspec(1, ff), wspec(1, ff),
            wspec(ff, d), wspec(1, d),
        ],
        out_specs=pl.BlockSpec((bm, d), lambda i, be: (i, 0)),
    )
    return pl.pallas_call(
        _expert_kernel,
        grid_spec=grid_spec,
        out_shape=jax.ShapeDtypeStruct((n_rows, d), F32),
        compiler_params=pltpu.CompilerParams(
            dimension_semantics=("arbitrary",), vmem_limit_bytes=V7X_VMEM_LIMIT),
        name="moe_experts",
    )(blk_e, xs, w_glu, w_lin, b_glu, b_lin, w_dn, b_dn)


def _combine_kernel(dest_hbm, y_hbm, x_ref, gate_ref, lg_ref, lbias_ref, o_ref,
                    dest_smem, rows_ref, sem_idx, sem_rows, *, dn_alpha):
    t = x_ref.shape[0]
    i = pl.program_id(0)
    idx_copy = pltpu.make_async_copy(dest_hbm.at[i], dest_smem, sem_idx)
    idx_copy.start()
    idx_copy.wait()

    def issue(tt, carry):
        for kk in range(TOP_K):
            _row_copy(y_hbm, dest_smem[kk * t + tt], rows_ref.at[kk], tt, sem_rows).start()
        return carry

    lax.fori_loop(0, t, issue, 0, unroll=8)

    def drain(tt, carry):
        for kk in range(TOP_K):
            _row_copy(y_hbm, 0, rows_ref.at[kk], 0, sem_rows).wait()
        return carry

    lax.fori_loop(0, t, drain, 0, unroll=8)

    gate = gate_ref[...]
    ffn = rows_ref[0] * gate[:, 0:1]
    for kk in range(1, TOP_K):
        ffn = ffn + rows_ref[kk] * gate[:, kk:kk + 1]
    o_ref[...] = _layer_norm(dn_alpha * x_ref[...] + ffn, lg_ref[...], lbias_ref[...])


def _combine(y, x2, dest_blocks, gate_t, ln_g, ln_b, dn_alpha):
    n, d = x2.shape
    t = dest_blocks.shape[1] // TOP_K
    vec = pl.BlockSpec((1, d), lambda i: (0, 0))
    return pl.pallas_call(
        functools.partial(_combine_kernel, dn_alpha=dn_alpha),
        grid=(n // t,),
        in_specs=[
            pl.BlockSpec(memory_space=pl.ANY),
            pl.BlockSpec(memory_space=pl.ANY),
            pl.BlockSpec((t, d), lambda i: (i, 0)),
            pl.BlockSpec((t, TOP_K), lambda i: (i, 0)),
            vec, vec,
        ],
        out_specs=pl.BlockSpec((t, d), lambda i: (i, 0)),
        out_shape=jax.ShapeDtypeStruct((n, d), F32),
        scratch_shapes=[
            pltpu.SMEM((TOP_K * t,), jnp.int32),
            pltpu.VMEM((TOP_K, t, d), F32),
            pltpu.SemaphoreType.DMA,
            pltpu.SemaphoreType.DMA,
        ],
        compiler_params=pltpu.CompilerParams(dimension_semantics=("arbitrary",)),
        name="moe_combine",
    )(dest_blocks, y, x2, gate_t, ln_g, ln_b)


def _moe_layer(x2, w_r, b_r, w_glu, w_lin, b_glu, b_lin, w_dn, b_dn, ln_g, ln_b, dn_alpha):
    n, d = x2.shape
    e = w_r.shape[1]
    idx, gate, rank, counts = _router(x2, w_r.T, b_r.reshape(e, 1))
    counts = counts[:, 0]
    padded = (counts + EXPERT_ROWS - 1) // EXPERT_ROWS * EXPERT_ROWS
    pend = jnp.cumsum(padded)
    pstart = pend - padded
    n_blocks = -(-(n * TOP_K) // EXPERT_ROWS) + e
    n_rows = n_blocks * EXPERT_ROWS
    block_starts = jnp.arange(n_blocks, dtype=jnp.int32) * EXPERT_ROWS
    blk_e = jnp.minimum(jnp.searchsorted(pend, block_starts, side='right'), e - 1).astype(jnp.int32)
    dest = pstart[idx] + rank
    t = min(MOVE_TILE, n)
    dest_blocks = dest.reshape(TOP_K, n // t, t).transpose(1, 0, 2).reshape(n // t, TOP_K * t)
    xs = _dispatch(x2, dest_blocks, n_rows)
    y = _experts(xs, blk_e, w_glu, w_lin, b_glu, b_lin, w_dn, b_dn)
    return _combine(y, x2, dest_blocks, gate.T, ln_g, ln_b, dn_alpha)


def kernel(x, hgrn_w_in, hgrn_norm_g, hgrn_w_out, lower_bounds, ret_w_in, ret_w_out, ln_mix_g, ln_mix_b,
           ln_ffn_g, ln_ffn_b, router_w, router_b, expert_w_gu, expert_b_gu, expert_w_down, expert_b_down):
    b, s, d = x.shape
    depth = lower_bounds.shape[0]
    dn_alpha = (2 * depth) ** 0.25
    lb_soft = jax.nn.softmax(lower_bounds.astype(F32), axis=0)
    lbs = jnp.cumsum(lb_soft, axis=0) - lb_soft[0]
    cos, sin_e, sin_o = _rotary_tables(s, d // RET_HEADS)
    row = lambda p, i: p[i].reshape(1, -1)
    for i in range(depth):
        j = i // 2
        if i % 2 == 0:
            x = _hgrn_layer(x, hgrn_w_in[j].astype(BF16), row(lbs, i), row(hgrn_norm_g, j),
                            hgrn_w_out[j].astype(BF16), row(ln_mix_g, i), row(ln_mix_b, i), dn_alpha)
        else:
            x = _ret_layer(x, ret_w_in[j].astype(BF16), cos, sin_e, sin_o, ret_w_out[j].astype(BF16),
                           row(ln_mix_g, i), row(ln_mix_b, i), dn_alpha)
        w_gu = expert_w_gu[i]
        b_gu = expert_b_gu[i]
        x2 = _moe_layer(
            x.reshape(b * s, d), router_w[i], router_b[i],
            w_gu[:, :, 0::2].astype(BF16), w_gu[:, :, 1::2].astype(BF16),
            b_gu[:, None, 0::2], b_gu[:, None, 1::2],
            expert_w_down[i].astype(BF16), expert_b_down[i][:, None, :],
            row(ln_ffn_g, i), row(ln_ffn_b, i), dn_alpha)
        x = x2.reshape(b, s, d)
    return x
```

```python
import functools
import math

import jax
import jax.numpy as jnp
from jax import lax
from jax.experimental import pallas as pl
from jax.experimental.pallas import tpu as pltpu

HGRN_HEAD = 128
RET_HEADS = 4
ROPE_BASE = 10000.0
N_EXPERTS = 32
TOP_K = 4
SWIGLU_LIMIT = 7.0
SWIGLU_ALPHA = 1.702
LN_EPS = 1e-5
RMS_EPS = 1e-6

V7X_VMEM_LIMIT = 56 * 1024 * 1024
SEQ_TILE = 256
HGRN_CHUNK = 64
HGRN_SUB = 16
RET_CHUNK = 128
ROUTER_TILE = 512
EXPERT_ROWS = 256
MOVE_TILE = 256
PLAN_TILE = 2048
WEIGHT_SPLIT_COLS = 256

F32 = jnp.float32
BF16 = jnp.bfloat16

_NT = (((1,), (1,)), ((), ()))
_TN = (((0,), (0,)), ((), ()))


def _dot(a, b, dims=None):
    if dims is None:
        return jnp.dot(a, b, preferred_element_type=F32)
    return lax.dot_general(a, b, dims, preferred_element_type=F32)


def _layer_norm(v, g, b):
    mu = jnp.mean(v, axis=-1, keepdims=True)
    d = v - mu
    var = jnp.mean(d * d, axis=-1, keepdims=True)
    return d * lax.rsqrt(var + LN_EPS) * g + b


def _split3(v):
    hi = v.astype(BF16)
    r1 = v - hi.astype(F32)
    mid = r1.astype(BF16)
    lo = (r1 - mid.astype(F32)).astype(BF16)
    return hi, mid, lo


def _hgrn_kernel(x_ref, w_in_ref, lb_ref, ng_ref, w_out_ref, lg_ref, lbias_ref,
                 o_ref, h_ref, oacc_ref, og_ref, st_ref, qa_ref, k_ref, b_ref, *, dn_alpha):
    ts, d = x_ref.shape[1], x_ref.shape[2]
    n_heads = d // HGRN_HEAD
    c = HGRN_CHUNK
    n_sub = c // HGRN_SUB

    @pl.when(pl.program_id(1) == 0)
    def _():
        st_ref[...] = jnp.zeros_like(st_ref)

    h_ref[...] = _dot(x_ref[0].astype(BF16), w_in_ref[...])

    sub = HGRN_SUB
    row = lax.broadcasted_iota(jnp.int32, (c, c), 0)
    col = lax.broadcasted_iota(jnp.int32, (c, c), 1)
    sub_shift = sub.bit_length() - 1
    same_sub = (row >> sub_shift) == (col >> sub_shift)
    tril = jnp.where(same_sub & (row >= col), 1.0, 0.0).astype(BF16)

    def gates(rows):
        lb = lb_ref[...]
        one_m_lb = 1.0 - lb
        q = h_ref[rows, 0:d]
        sf = h_ref[rows, d:2 * d]
        qa_ref[...] = q * jax.nn.sigmoid(q)
        k_ref[...] = one_m_lb * jax.nn.sigmoid(-sf)
        logf = jnp.log(lb + one_m_lb * jax.nn.sigmoid(sf))
        hi = logf.astype(BF16)
        lo = (logf - hi.astype(F32)).astype(BF16)
        b_ref[...] = _dot(tril, lo) + _dot(tril, hi)

    def head_scores(rows, hd):
        lo_, hi_ = HGRN_HEAD * hd, HGRN_HEAD * (hd + 1)
        b = b_ref[:, lo_:hi_]
        vb = h_ref[rows, 2 * d + lo_:2 * d + hi_].astype(BF16)
        tot = [b[sub * j + sub - 1:sub * j + sub, :] for j in range(n_sub)]
        to_end = jnp.concatenate([jnp.broadcast_to(t, (sub, HGRN_HEAD)) for t in tot], axis=0) - b
        q_dec = qa_ref[:, lo_:hi_] * jnp.exp(b)
        k_st = k_ref[:, lo_:hi_] * jnp.exp(to_end)
        g_start = [jnp.zeros_like(tot[0])]
        for j in range(n_sub):
            g_start.append(g_start[j] + tot[j])
        g_last = g_start[n_sub]
        qd = [q_dec[sub * j:sub * (j + 1), :] for j in range(n_sub)]
        ks = [k_st[sub * j:sub * (j + 1), :] for j in range(n_sub)]
        q_in = jnp.concatenate([qd[j] * jnp.exp(g_start[j]) for j in range(n_sub)], axis=0).astype(BF16)
        k_out = jnp.concatenate([ks[j] * jnp.exp(g_last - g_start[j + 1]) for j in range(n_sub)],
                                axis=0).astype(BF16)
        st = st_ref[hd]
        o_inter = _dot(q_in, st.astype(BF16), _NT)
        scs = []
        for i in range(n_sub):
            qm = jnp.concatenate([qd[j] * jnp.exp(g_start[j] - g_start[i + 1]) for j in range(i, n_sub)],
                                 axis=0).astype(BF16)
            scs.append(_dot(qm, ks[i].astype(BF16), _NT))
        new_st = st * jnp.exp(g_last) + _dot(vb, k_out, _TN)
        return o_inter, scs, vb, new_st

    def head_output(rows, hd, o_h, scs, vb, new_st):
        lo_, hi_ = HGRN_HEAD * hd, HGRN_HEAD * (hd + 1)
        for i, sc in enumerate(scs):
            rr = lax.broadcasted_iota(jnp.int32, sc.shape, 0)
            cc = lax.broadcasted_iota(jnp.int32, sc.shape, 1)
            sc = jnp.where(rr >= cc, sc, 0.0).astype(BF16)
            part = _dot(sc, vb[sub * i:sub * (i + 1), :])
            if i:
                part = jnp.concatenate([jnp.zeros((sub * i, HGRN_HEAD), F32), part], axis=0)
            o_h = o_h + part
        st_ref[hd] = new_st
        oacc_ref[rows, lo_:hi_] = o_h

    strip = 16

    def chunk_body(ci, carry):
        r0 = pl.multiple_of(ci * c, c)
        rows = pl.ds(r0, c)
        gates(rows)
        staged = [head_scores(rows, hd) for hd in range(n_heads)]
        for hd in range(n_heads):
            head_output(rows, hd, *staged[hd])
        for si in range(c // strip):
            rs = pl.ds(pl.multiple_of(r0 + si * strip, strip), strip)
            o = oacc_ref[rs, :]
            g = h_ref[rs, 3 * d:4 * d]
            o = o * lax.rsqrt(jnp.mean(o * o, axis=-1, keepdims=True) + RMS_EPS) * ng_ref[...]
            og_ref[rs, :] = (o * (g * jax.nn.sigmoid(g))).astype(BF16)
        return carry

    lax.fori_loop(0, ts // c, chunk_body, 0)

    mix = _dot(og_ref[...], w_out_ref[...])
    o_ref[0] = _layer_norm(dn_alpha * x_ref[0] + mix, lg_ref[...], lbias_ref[...])


def _hgrn_layer(x, w_in, lb, norm_g, w_out, ln_g, ln_b, dn_alpha):
    b, s, d = x.shape
    ts = min(SEQ_TILE, s)
    vec = pl.BlockSpec((1, d), lambda bi, si: (0, 0))
    return pl.pallas_call(
        functools.partial(_hgrn_kernel, dn_alpha=dn_alpha),
        grid=(b, s // ts),
        in_specs=[
            pl.BlockSpec((1, ts, d), lambda bi, si: (bi, si, 0)),
            pl.BlockSpec((d, 4 * d), lambda bi, si: (0, 0)),
            vec, vec,
            pl.BlockSpec((d, d), lambda bi, si: (0, 0)),
            vec, vec,
        ],
        out_specs=pl.BlockSpec((1, ts, d), lambda bi, si: (bi, si, 0)),
        out_shape=jax.ShapeDtypeStruct((b, s, d), F32),
        scratch_shapes=[
            pltpu.VMEM((ts, 4 * d), F32),
            pltpu.VMEM((ts, d), F32),
            pltpu.VMEM((ts, d), BF16),
            pltpu.VMEM((d // HGRN_HEAD, HGRN_HEAD, HGRN_HEAD), F32),
            pltpu.VMEM((HGRN_CHUNK, d), F32),
            pltpu.VMEM((HGRN_CHUNK, d), F32),
            pltpu.VMEM((HGRN_CHUNK, d), F32),
        ],
        compiler_params=pltpu.CompilerParams(
            dimension_semantics=("arbitrary", "arbitrary"),
            vmem_limit_bytes=V7X_VMEM_LIMIT),
        name="hgrn_mixer",
    )(x, w_in, lb, norm_g, w_out, ln_g, ln_b)


def _ret_kernel(x_ref, w_in_ref, cos_ref, sin_e_ref, sin_o_ref, w_out_ref, lg_ref, lbias_ref,
                o_ref, h_ref, og_ref, st_ref, *, dn_alpha):
    ts, d = x_ref.shape[1], x_ref.shape[2]
    qk = d
    vd = 2 * d
    hk = qk // RET_HEADS
    hv = vd // RET_HEADS
    c = RET_CHUNK

    @pl.when(pl.program_id(1) == 0)
    def _():
        st_ref[...] = jnp.zeros_like(st_ref)

    h_ref[...] = _dot(x_ref[0].astype(BF16), w_in_ref[...])

    ri = lax.broadcasted_iota(jnp.int32, (c, c), 0)
    cj = lax.broadcasted_iota(jnp.int32, (c, c), 1)
    rel = (ri - cj).astype(F32)
    idx_col = lax.broadcasted_iota(jnp.int32, (c, 1), 0).astype(F32)

    def chunk_body(ci, carry):
        r0 = pl.multiple_of(ci * c, c)
        rows = pl.ds(r0, c)
        cos = jnp.tile(cos_ref[rows, :], (1, RET_HEADS))
        sin_e = jnp.tile(sin_e_ref[rows, :], (1, RET_HEADS))
        sin_o = jnp.tile(sin_o_ref[rows, :], (1, RET_HEADS))

        def rope(t):
            nxt = pltpu.roll(t, qk - 1, 1)
            prv = pltpu.roll(t, 1, 1)
            return t * cos + nxt * sin_e + prv * sin_o

        q = rope(h_ref[rows, 0:qk]).astype(BF16)
        k = rope(h_ref[rows, qk:2 * qk]) * (hk ** -0.5)
        v = h_ref[rows, 2 * qk:2 * qk + vd].astype(BF16)
        lgam = [math.log(1.0 - 2.0 ** (-5.0 - hd)) for hd in range(RET_HEADS)]
        staged = []
        for hd in range(RET_HEADS):
            zeta = jnp.exp((c - 1.0 - idx_col) * lgam[hd])
            q_h = q[:, hk * hd:hk * (hd + 1)]
            k_h = k[:, hk * hd:hk * (hd + 1)]
            v_h = v[:, hv * hd:hv * (hd + 1)]
            st = st_ref[hd]
            sc = _dot(q_h, k_h.astype(BF16), _NT)
            o_inter = _dot(q_h, st.astype(BF16))
            new_st = st * math.exp(c * lgam[hd]) + _dot((k_h * zeta).astype(BF16), v_h, _TN)
            staged.append((sc, o_inter, new_st, v_h))
        for hd, (sc, o_inter, new_st, v_h) in enumerate(staged):
            decay = jnp.where(rel >= 0, jnp.exp(rel * lgam[hd]), 0.0)
            xi = jnp.exp((idx_col + 1.0) * lgam[hd])
            o_h = _dot((sc * decay).astype(BF16), v_h) + o_inter * xi
            st_ref[hd] = new_st
            mu = jnp.mean(o_h, axis=-1, keepdims=True)
            dlt = o_h - mu
            var = jnp.mean(dlt * dlt, axis=-1, keepdims=True)
            g_h = h_ref[rows, 2 * qk + vd + hv * hd:2 * qk + vd + hv * (hd + 1)]
            og_ref[rows, hv * hd:hv * (hd + 1)] = (
                (g_h * jax.nn.sigmoid(g_h)) * (dlt * lax.rsqrt(var + LN_EPS))).astype(BF16)
        return carry

    lax.fori_loop(0, ts // c, chunk_body, 0)

    mix = _dot(og_ref[...], w_out_ref[...])
    o_ref[0] = _layer_norm(dn_alpha * x_ref[0] + mix, lg_ref[...], lbias_ref[...])


def _ret_layer(x, w_in, cos, sin_e, sin_o, w_out, ln_g, ln_b, dn_alpha):
    b, s, d = x.shape
    ts = min(SEQ_TILE, s)
    n_in = w_in.shape[1]
    hk = d // RET_HEADS
    vec = pl.BlockSpec((1, d), lambda bi, si: (0, 0))
    tab = pl.BlockSpec((ts, hk), lambda bi, si: (si, 0))
    return pl.pallas_call(
        functools.partial(_ret_kernel, dn_alpha=dn_alpha),
        grid=(b, s // ts),
        in_specs=[
            pl.BlockSpec((1, ts, d), lambda bi, si: (bi, si, 0)),
            pl.BlockSpec((d, n_in), lambda bi, si: (0, 0)),
            tab, tab, tab,
            pl.BlockSpec((2 * d, d), lambda bi, si: (0, 0)),
            vec, vec,
        ],
        out_specs=pl.BlockSpec((1, ts, d), lambda bi, si: (bi, si, 0)),
        out_shape=jax.ShapeDtypeStruct((b, s, d), F32),
        scratch_shapes=[
            pltpu.VMEM((ts, n_in), F32),
            pltpu.VMEM((ts, 2 * d), BF16),
            pltpu.VMEM((RET_HEADS, hk, 2 * d // RET_HEADS), F32),
        ],
        compiler_params=pltpu.CompilerParams(
            dimension_semantics=("arbitrary", "arbitrary"),
            vmem_limit_bytes=V7X_VMEM_LIMIT),
        name="ret_mixer",
    )(x, w_in, cos, sin_e, sin_o, w_out, ln_g, ln_b)


def _rotary_tables(seq_len, hk):
    pos = jnp.arange(seq_len, dtype=F32)
    inv = 1.0 / (ROPE_BASE ** jnp.linspace(0.0, 1.0, hk // 2, dtype=F32))
    theta = pos[:, None] * jnp.repeat(inv, 2)[None, :]
    cos, sin = jnp.cos(theta), jnp.sin(theta)
    even = (jnp.arange(hk) % 2 == 0)[None, :]
    return cos, jnp.where(even, -sin, 0.0), jnp.where(even, 0.0, sin)


def _router_kernel(x_ref, w_ref, b_ref, idx_ref, gate_ref, rank_ref, cnt_ref, acc_ref):
    t = x_ref.shape[0]
    e = w_ref.shape[0]

    @pl.when(pl.program_id(0) == 0)
    def _():
        acc_ref[...] = jnp.zeros_like(acc_ref)

    xh, xm, xl = _split3(x_ref[...])
    wh, wm, wl = _split3(w_ref[...])
    logits = (_dot(wl, xh, _NT) + _dot(wh, xl, _NT) + _dot(wm, xm, _NT)
              + _dot(wm, xh, _NT) + _dot(wh, xm, _NT) + _dot(wh, xh, _NT)) + b_ref[...]

    eid = lax.broadcasted_iota(jnp.int32, (e, t), 0)
    ti = lax.broadcasted_iota(jnp.int32, (t, t), 0)
    tj = lax.broadcasted_iota(jnp.int32, (t, t), 1)
    before = jnp.where(ti < tj, 1.0, 0.0).astype(BF16)
    work = logits
    vals, idxs, ranks = [], [], []
    base = acc_ref[...]
    for _ in range(TOP_K):
        m = jnp.max(work, axis=0, keepdims=True)
        sel = jnp.min(jnp.where(work == m, eid, e), axis=0, keepdims=True)
        hit = eid == sel
        work = jnp.where(hit, -jnp.inf, work)
        onehot = jnp.where(hit, 1.0, 0.0)
        excl = _dot(onehot.astype(BF16), before)
        rank = jnp.sum(onehot * (excl + base), axis=0, keepdims=True)
        base = base + jnp.sum(onehot, axis=1, keepdims=True)
        vals.append(m)
        idxs.append(sel)
        ranks.append(rank)
    acc_ref[...] = base
    cnt_ref[...] = base.astype(jnp.int32)
    ex = [jnp.exp(vv - vals[0]) for vv in vals]
    den = ex[0] + ex[1] + ex[2] + ex[3]
    gate_ref[...] = jnp.concatenate(ex, axis=0) / den
    idx_ref[...] = jnp.concatenate(idxs, axis=0)
    rank_ref[...] = jnp.concatenate(ranks, axis=0).astype(jnp.int32)


def _router(x2, w_t, b_col):
    n, d = x2.shape
    e = w_t.shape[0]
    t = min(ROUTER_TILE, n)
    kt = pl.BlockSpec((TOP_K, t), lambda i: (0, i))
    return pl.pallas_call(
        _router_kernel,
        grid=(n // t,),
        in_specs=[
            pl.BlockSpec((t, d), lambda i: (i, 0)),
            pl.BlockSpec((e, d), lambda i: (0, 0)),
            pl.BlockSpec((e, 1), lambda i: (0, 0)),
        ],
        out_specs=[kt, kt, kt, pl.BlockSpec((e, 1), lambda i: (0, 0))],
        out_shape=[
            jax.ShapeDtypeStruct((TOP_K, n), jnp.int32),
            jax.ShapeDtypeStruct((TOP_K, n), F32),
            jax.ShapeDtypeStruct((TOP_K, n), jnp.int32),
            jax.ShapeDtypeStruct((e, 1), jnp.int32),
        ],
        scratch_shapes=[pltpu.VMEM((e, 1), F32)],
        compiler_params=pltpu.CompilerParams(dimension_semantics=("arbitrary",)),
        name="moe_router",
    )(x2, w_t, b_col)


def _plan_kernel(cnt_ref, idx_ref, rank_ref, dest_ref, blk_ref, fill_ref, *, move_tile):
    e = cnt_ref.shape[0]
    pt = idx_ref.shape[1]
    shift = EXPERT_ROWS.bit_length() - 1
    cnt_col = cnt_ref[...]
    pad_col = (((cnt_col + (EXPERT_ROWS - 1)) >> shift) << shift).astype(F32)
    r = lax.broadcasted_iota(jnp.int32, (e, e), 0)
    c = lax.broadcasted_iota(jnp.int32, (e, e), 1)
    pad_row = jnp.sum(jnp.where(r == c, pad_col, 0.0), axis=0, keepdims=True)
    start_col = jnp.sum(jnp.where(c < r, pad_row, 0.0), axis=1, keepdims=True)
    end_col = start_col + pad_col
    used_col = start_col + cnt_col.astype(F32)

    nb = blk_ref.shape[1]
    starts = (lax.broadcasted_iota(jnp.int32, (e, nb), 1) * EXPERT_ROWS).astype(F32)
    owner = jnp.sum(jnp.where(end_col <= starts, 1.0, 0.0), axis=0, keepdims=True)
    blk_ref[...] = jnp.minimum(owner, e - 1.0).astype(jnp.int32)

    lane = lax.broadcasted_iota(jnp.int32, (e, fill_ref.shape[1]), 1)
    sub = lax.broadcasted_iota(jnp.int32, (e, fill_ref.shape[1]), 0)
    fill = jnp.where(lane == sub, used_col, 0.0) + jnp.where(lane == sub + e, end_col, 0.0)
    fill_ref[...] = jnp.sum(fill, axis=0, keepdims=True).astype(jnp.int32)

    eid = lax.broadcasted_iota(jnp.int32, (e, pt), 0)
    for kk in range(TOP_K):
        hit = eid == idx_ref[kk:kk + 1, :]
        base = jnp.sum(jnp.where(hit, start_col, 0.0), axis=0, keepdims=True)
        d_k = base.astype(jnp.int32) + rank_ref[kk:kk + 1, :]
        for j in range(pt // move_tile):
            dest_ref[j, :, kk * move_tile:(kk + 1) * move_tile] = d_k[:, j * move_tile:(j + 1) * move_tile]


def _plan(counts, idx, rank, n_blocks, move_tile):
    e = counts.shape[0]
    n = idx.shape[1]
    pt = min(PLAN_TILE, n)
    kt = pl.BlockSpec((TOP_K, pt), lambda i: (0, i))
    fill_lanes = 128 * (-(-2 * e // 128))
    return pl.pallas_call(
        functools.partial(_plan_kernel, move_tile=move_tile),
        grid=(n // pt,),
        in_specs=[pl.BlockSpec((e, 1), lambda i: (0, 0)), kt, kt],
        out_specs=[
            pl.BlockSpec((pt // move_tile, 1, TOP_K * move_tile), lambda i: (i, 0, 0)),
            pl.BlockSpec((1, n_blocks), lambda i: (0, 0)),
            pl.BlockSpec((1, fill_lanes), lambda i: (0, 0)),
        ],
        out_shape=[
            jax.ShapeDtypeStruct((n // move_tile, 1, TOP_K * move_tile), jnp.int32),
            jax.ShapeDtypeStruct((1, n_blocks), jnp.int32),
            jax.ShapeDtypeStruct((1, fill_lanes), jnp.int32),
        ],
        compiler_params=pltpu.CompilerParams(dimension_semantics=("arbitrary",)),
        name="moe_plan",
    )(counts, idx, rank)


def _row_copy(src_ref, src_row, dst_ref, dst_row, sem):
    return pltpu.make_async_copy(src_ref.at[pl.ds(src_row, 1)], dst_ref.at[pl.ds(dst_row, 1)], sem)


def _fetch_move_indices(dest_hbm, dest_smem, sem_idx):
    i = pl.program_id(0)
    slot = i % 2

    def copy(step, s):
        return pltpu.make_async_copy(dest_hbm.at[step, 0], dest_smem.at[s], sem_idx.at[s])

    @pl.when(i == 0)
    def _():
        copy(0, 0).start()

    @pl.when(i + 1 < pl.num_programs(0))
    def _():
        copy(i + 1, 1 - slot).start()

    copy(i, slot).wait()
    return slot


def _dispatch_kernel(fill_ref, dest_hbm, x_ref, xs_out, dest_smem, zero_ref, sem_idx, sem_rows, sem_zero,
                     *, n_experts):
    t = x_ref.shape[0]
    n_rows = xs_out.shape[0]
    i = pl.program_id(0)

    @pl.when(i == 0)
    def _():
        zero_ref[...] = jnp.zeros_like(zero_ref)
        tail_blk = fill_ref[2 * n_experts - 1] // EXPERT_ROWS

        def blk_copy(bi):
            r0 = pl.multiple_of(bi * EXPERT_ROWS, EXPERT_ROWS)
            return pltpu.make_async_copy(zero_ref, xs_out.at[pl.ds(r0, EXPERT_ROWS)], sem_zero)

        def pad_rows(fn):
            def per_expert(ee, carry):
                def one(rr, c2):
                    fn(_row_copy(zero_ref, 0, xs_out, rr, sem_zero))
                    return c2
                lax.fori_loop(fill_ref[ee], fill_ref[n_experts + ee], one, 0)
                return carry
            lax.fori_loop(0, n_experts, per_expert, 0)

            def per_tail(bi, carry):
                fn(blk_copy(bi))
                return carry
            lax.fori_loop(tail_blk, n_rows // EXPERT_ROWS, per_tail, 0)

        pad_rows(lambda cp: cp.start())
        pad_rows(lambda cp: cp.wait())

    slot = _fetch_move_indices(dest_hbm, dest_smem, sem_idx)

    def issue(tt, carry):
        for kk in range(TOP_K):
            _row_copy(x_ref, tt, xs_out, dest_smem[slot, kk * t + tt], sem_rows).start(priority=kk % 2)
        return carry

    lax.fori_loop(0, t, issue, 0, unroll=8)

    def drain(tt, carry):
        for kk in range(TOP_K):
            _row_copy(x_ref, 0, xs_out, 0, sem_rows).wait()
        return carry

    lax.fori_loop(0, t, drain, 0, unroll=8)


def _dispatch(x2, dest_blocks, fill, n_rows, n_experts):
    n, d = x2.shape
    t = dest_blocks.shape[2] // TOP_K
    grid_spec = pltpu.PrefetchScalarGridSpec(
        num_scalar_prefetch=1,
        grid=(n // t,),
        in_specs=[
            pl.BlockSpec(memory_space=pl.ANY),
            pl.BlockSpec((t, d), lambda i, f: (i, 0)),
        ],
        out_specs=pl.BlockSpec(memory_space=pl.ANY),
        scratch_shapes=[
            pltpu.SMEM((2, TOP_K * t), jnp.int32),
            pltpu.VMEM((EXPERT_ROWS, d), x2.dtype),
            pltpu.SemaphoreType.DMA((2,)),
            pltpu.SemaphoreType.DMA,
            pltpu.SemaphoreType.DMA,
        ],
    )
    return pl.pallas_call(
        functools.partial(_dispatch_kernel, n_experts=n_experts),
        grid_spec=grid_spec,
        out_shape=jax.ShapeDtypeStruct((n_rows, d), x2.dtype),
        compiler_params=pltpu.CompilerParams(dimension_semantics=("arbitrary",)),
        name="moe_dispatch",
    )(fill, dest_blocks, x2)


def _expert_kernel(blk_e_ref, xs_ref, wgu_ref, bg_ref, bl_ref, wd_ref, bd_ref, y_ref,
                   wt_ref, wg_s, wl_s, wd_s):
    i = pl.program_id(0)
    ff, d = wg_s.shape
    tc, rk = wt_ref.shape
    half = tc // 2
    changed = blk_e_ref[i] != blk_e_ref[jnp.maximum(i - 1, 0)]

    @pl.when((i == 0) | changed)
    def _():
        for kb in range(d // rk):
            for j in range(2 * ff // tc):
                wt_ref[...] = wgu_ref[0, rk * kb:rk * (kb + 1), tc * j:tc * (j + 1)].T
                wg_s[half * j:half * (j + 1), rk * kb:rk * (kb + 1)] = (
                    wt_ref[pl.ds(0, half, stride=2), :].astype(BF16))
                wl_s[half * j:half * (j + 1), rk * kb:rk * (kb + 1)] = (
                    wt_ref[pl.ds(1, half, stride=2), :].astype(BF16))
        wd_s[...] = wd_ref[0].astype(BF16)

    xb = xs_ref[...].astype(BF16)
    hg = _dot(xb, wg_s[...], _NT) + bg_ref[0]
    hl = _dot(xb, wl_s[...], _NT) + bl_ref[0]
    glu = jnp.minimum(hg, SWIGLU_LIMIT)
    lin = jnp.clip(hl, -SWIGLU_LIMIT, SWIGLU_LIMIT)
    a = glu * jax.nn.sigmoid(SWIGLU_ALPHA * glu) * (lin + 1.0)
    y_ref[...] = _dot(a.astype(BF16), wd_s[...]) + bd_ref[0]


def _experts(xs, blk_e, w_gu, b_glu, b_lin, w_dn, b_dn):
    n_rows, d = xs.shape
    ff = w_dn.shape[1]
    bm = EXPERT_ROWS
    wspec = lambda r, c: pl.BlockSpec((1, r, c), lambda i, be: (be[i], 0, 0))
    grid_spec = pltpu.PrefetchScalarGridSpec(
        num_scalar_prefetch=1,
        grid=(n_rows // bm,),
        in_specs=[
            pl.BlockSpec((bm, d), lambda i, be: (i, 0)),
            wspec(d, 2 * ff), wspec(1, ff), wspec(1, ff),
            wspec(ff, d), wspec(1, d),
        ],
        out_specs=pl.BlockSpec((bm, d), lambda i, be: (i, 0)),
        scratch_shapes=[
            pltpu.VMEM((WEIGHT_SPLIT_COLS, 128), F32),
            pltpu.VMEM((ff, d), BF16),
            pltpu.VMEM((ff, d), BF16),
            pltpu.VMEM((ff, d), BF16),
        ],
    )
    return pl.pallas_call(
        _expert_kernel,
        grid_spec=grid_spec,
        out_shape=jax.ShapeDtypeStruct((n_rows, d), F32),
        compiler_params=pltpu.CompilerParams(
            dimension_semantics=("arbitrary",), vmem_limit_bytes=V7X_VMEM_LIMIT),
        name="moe_experts",
    )(blk_e, xs, w_gu, b_glu, b_lin, w_dn, b_dn)


def _combine_kernel(dest_hbm, y_hbm, x_ref, gate_ref, lg_ref, lbias_ref, o_ref,
                    dest_smem, rows_ref, sem_idx, sem_rows, *, dn_alpha):
    t = x_ref.shape[0]
    slot = _fetch_move_indices(dest_hbm, dest_smem, sem_idx)

    def issue(tt, carry):
        for kk in range(TOP_K):
            _row_copy(y_hbm, dest_smem[slot, kk * t + tt], rows_ref.at[kk], tt, sem_rows).start(priority=kk % 2)
        return carry

    lax.fori_loop(0, t, issue, 0, unroll=8)

    def drain(tt, carry):
        for kk in range(TOP_K):
            _row_copy(y_hbm, 0, rows_ref.at[kk], 0, sem_rows).wait()
        return carry

    lax.fori_loop(0, t, drain, 0, unroll=8)

    gate = gate_ref[...]
    ffn = rows_ref[0] * gate[:, 0:1]
    for kk in range(1, TOP_K):
        ffn = ffn + rows_ref[kk] * gate[:, kk:kk + 1]
    o_ref[...] = _layer_norm(dn_alpha * x_ref[...] + ffn, lg_ref[...], lbias_ref[...])


def _combine(y, x2, dest_blocks, gate_t, ln_g, ln_b, dn_alpha):
    n, d = x2.shape
    t = dest_blocks.shape[2] // TOP_K
    vec = pl.BlockSpec((1, d), lambda i: (0, 0))
    return pl.pallas_call(
        functools.partial(_combine_kernel, dn_alpha=dn_alpha),
        grid=(n // t,),
        in_specs=[
            pl.BlockSpec(memory_space=pl.ANY),
            pl.BlockSpec(memory_space=pl.ANY),
            pl.BlockSpec((t, d), lambda i: (i, 0)),
            pl.BlockSpec((t, TOP_K), lambda i: (i, 0)),
            vec, vec,
        ],
        out_specs=pl.BlockSpec((t, d), lambda i: (i, 0)),
        out_shape=jax.ShapeDtypeStruct((n, d), F32),
        scratch_shapes=[
            pltpu.SMEM((2, TOP_K * t), jnp.int32),
            pltpu.VMEM((TOP_K, t, d), F32),
            pltpu.SemaphoreType.DMA((2,)),
            pltpu.SemaphoreType.DMA,
        ],
        compiler_params=pltpu.CompilerParams(dimension_semantics=("arbitrary",)),
        name="moe_combine",
    )(dest_blocks, y, x2, gate_t, ln_g, ln_b)


def _moe_layer(x2, w_r, b_r, w_gu, b_glu, b_lin, w_dn, b_dn, ln_g, ln_b, dn_alpha):
    n, d = x2.shape
    e = w_r.shape[1]
    idx, gate, rank, counts = _router(x2, w_r.T, b_r.reshape(e, 1))
    n_blocks = -(-(n * TOP_K) // EXPERT_ROWS) + e
    dest_blocks, blk_e, fill = _plan(counts, idx, rank, n_blocks, min(MOVE_TILE, n))
    xs = _dispatch(x2, dest_blocks, fill.reshape(-1), n_blocks * EXPERT_ROWS, e)
    y = _experts(xs, blk_e.reshape(-1), w_gu, b_glu, b_lin, w_dn, b_dn)
    return _combine(y, x2, dest_blocks, gate.T, ln_g, ln_b, dn_alpha)


def kernel(x, hgrn_w_in, hgrn_norm_g, hgrn_w_out, lower_bounds, ret_w_in, ret_w_out, ln_mix_g, ln_mix_b,
           ln_ffn_g, ln_ffn_b, router_w, router_b, expert_w_gu, expert_b_gu, expert_w_down, expert_b_down):
    b, s, d = x.shape
    depth = lower_bounds.shape[0]
    dn_alpha = (2 * depth) ** 0.25
    lb_soft = jax.nn.softmax(lower_bounds.astype(F32), axis=0)
    lbs = jnp.cumsum(lb_soft, axis=0) - lb_soft[0]
    cos, sin_e, sin_o = _rotary_tables(s, d // RET_HEADS)
    row = lambda p, i: p[i].reshape(1, -1)
    for i in range(depth):
        j = i // 2
        if i % 2 == 0:
            x = _hgrn_layer(x, hgrn_w_in[j].astype(BF16), row(lbs, i), row(hgrn_norm_g, j),
                            hgrn_w_out[j].astype(BF16), row(ln_mix_g, i), row(ln_mix_b, i), dn_alpha)
        else:
            x = _ret_layer(x, ret_w_in[j].astype(BF16), cos, sin_e, sin_o, ret_w_out[j].astype(BF16),
                           row(ln_mix_g, i), row(ln_mix_b, i), dn_alpha)
        b_gu = expert_b_gu[i]
        x2 = _moe_layer(
            x.reshape(b * s, d), router_w[i], router_b[i], expert_w_gu[i],
            b_gu[:, None, 0::2], b_gu[:, None, 1::2],
            expert_w_down[i], expert_b_down[i][:, None, :],
            row(ln_ffn_g, i), row(ln_ffn_b, i), dn_alpha)
        x = x2.reshape(b, s, d)
    return x
```

```python
import functools
import math

import jax
import jax.numpy as jnp
from jax import lax
from jax.experimental import pallas as pl
from jax.experimental.pallas import tpu as pltpu

HGRN_HEAD = 128
RET_HEADS = 4
ROPE_BASE = 10000.0
N_EXPERTS = 32
TOP_K = 4
SWIGLU_LIMIT = 7.0
SWIGLU_ALPHA = 1.702
LN_EPS = 1e-5
RMS_EPS = 1e-6

V7X_VMEM_LIMIT = 56 * 1024 * 1024
V7X_LANES = 128
SEQ_TILE = 256
HGRN_CHUNK = 64
HGRN_SUB = 16
RET_CHUNK = 128
ROUTER_TILE = 512
EXPERT_ROWS = 512
MOVE_TILE = 256
PLAN_TILE = 2048
WEIGHT_SPLIT_COLS = 256

F32 = jnp.float32
BF16 = jnp.bfloat16

_NT = (((1,), (1,)), ((), ()))
_TN = (((0,), (0,)), ((), ()))


def _dot(a, b, dims=None):
    if dims is None:
        return jnp.dot(a, b, preferred_element_type=F32)
    return lax.dot_general(a, b, dims, preferred_element_type=F32)


def _layer_norm(v, g, b):
    mu = jnp.mean(v, axis=-1, keepdims=True)
    d = v - mu
    var = jnp.mean(d * d, axis=-1, keepdims=True)
    return d * lax.rsqrt(var + LN_EPS) * g + b


def _split3(v):
    hi = v.astype(BF16)
    r1 = v - hi.astype(F32)
    mid = r1.astype(BF16)
    lo = (r1 - mid.astype(F32)).astype(BF16)
    return hi, mid, lo


def _hgrn_kernel(x_ref, w_in_ref, lb_ref, ng_ref, w_out_ref, lg_ref, lbias_ref,
                 o_ref, h_ref, oacc_ref, og_ref, st_ref, qa_ref, k_ref, b_ref, *, dn_alpha):
    ts, d = x_ref.shape[1], x_ref.shape[2]
    n_heads = d // HGRN_HEAD
    c = HGRN_CHUNK
    n_sub = c // HGRN_SUB

    @pl.when(pl.program_id(1) == 0)
    def _():
        st_ref[...] = jnp.zeros_like(st_ref)

    h_ref[...] = _dot(x_ref[0].astype(BF16), w_in_ref[...])

    sub = HGRN_SUB
    row = lax.broadcasted_iota(jnp.int32, (c, c), 0)
    col = lax.broadcasted_iota(jnp.int32, (c, c), 1)
    sub_shift = sub.bit_length() - 1
    same_sub = (row >> sub_shift) == (col >> sub_shift)
    tril = jnp.where(same_sub & (row >= col), 1.0, 0.0).astype(BF16)

    def gates(rows):
        lb = lb_ref[...]
        one_m_lb = 1.0 - lb
        q = h_ref[rows, 0:d]
        sf = h_ref[rows, d:2 * d]
        qa_ref[...] = q * jax.nn.sigmoid(q)
        k_ref[...] = one_m_lb * jax.nn.sigmoid(-sf)
        logf = jnp.log(lb + one_m_lb * jax.nn.sigmoid(sf))
        hi = logf.astype(BF16)
        lo = (logf - hi.astype(F32)).astype(BF16)
        b_ref[...] = _dot(tril, lo) + _dot(tril, hi)

    def head_scores(rows, hd):
        lo_, hi_ = HGRN_HEAD * hd, HGRN_HEAD * (hd + 1)
        b = b_ref[:, lo_:hi_]
        vb = h_ref[rows, 2 * d + lo_:2 * d + hi_].astype(BF16)
        tot = [b[sub * j + sub - 1:sub * j + sub, :] for j in range(n_sub)]
        to_end = jnp.concatenate([jnp.broadcast_to(t, (sub, HGRN_HEAD)) for t in tot], axis=0) - b
        q_dec = qa_ref[:, lo_:hi_] * jnp.exp(b)
        k_st = k_ref[:, lo_:hi_] * jnp.exp(to_end)
        g_start = [jnp.zeros_like(tot[0])]
        for j in range(n_sub):
            g_start.append(g_start[j] + tot[j])
        g_last = g_start[n_sub]
        qd = [q_dec[sub * j:sub * (j + 1), :] for j in range(n_sub)]
        ks = [k_st[sub * j:sub * (j + 1), :] for j in range(n_sub)]
        q_in = jnp.concatenate([qd[j] * jnp.exp(g_start[j]) for j in range(n_sub)], axis=0).astype(BF16)
        k_out = jnp.concatenate([ks[j] * jnp.exp(g_last - g_start[j + 1]) for j in range(n_sub)],
                                axis=0).astype(BF16)
        st = st_ref[hd]
        o_inter = _dot(q_in, st.astype(BF16), _NT)
        scs = []
        for i in range(n_sub):
            qm = jnp.concatenate([qd[j] * jnp.exp(g_start[j] - g_start[i + 1]) for j in range(i, n_sub)],
                                 axis=0).astype(BF16)
            scs.append(_dot(qm, ks[i].astype(BF16), _NT))
        new_st = st * jnp.exp(g_last) + _dot(vb, k_out, _TN)
        return o_inter, scs, vb, new_st

    def head_output(rows, hd, o_h, scs, vb, new_st):
        lo_, hi_ = HGRN_HEAD * hd, HGRN_HEAD * (hd + 1)
        for i, sc in enumerate(scs):
            rr = lax.broadcasted_iota(jnp.int32, sc.shape, 0)
            cc = lax.broadcasted_iota(jnp.int32, sc.shape, 1)
            sc = jnp.where(rr >= cc, sc, 0.0).astype(BF16)
            part = _dot(sc, vb[sub * i:sub * (i + 1), :])
            if i:
                part = jnp.concatenate([jnp.zeros((sub * i, HGRN_HEAD), F32), part], axis=0)
            o_h = o_h + part
        st_ref[hd] = new_st
        oacc_ref[rows, lo_:hi_] = o_h

    strip = 16

    def chunk_body(ci, carry):
        r0 = pl.multiple_of(ci * c, c)
        rows = pl.ds(r0, c)
        gates(rows)
        staged = [head_scores(rows, hd) for hd in range(n_heads)]
        for hd in range(n_heads):
            head_output(rows, hd, *staged[hd])
        for si in range(c // strip):
            rs = pl.ds(pl.multiple_of(r0 + si * strip, strip), strip)
            o = oacc_ref[rs, :]
            g = h_ref[rs, 3 * d:4 * d]
            o = o * lax.rsqrt(jnp.mean(o * o, axis=-1, keepdims=True) + RMS_EPS) * ng_ref[...]
            og_ref[rs, :] = (o * (g * jax.nn.sigmoid(g))).astype(BF16)
        return carry

    lax.fori_loop(0, ts // c, chunk_body, 0)

    mix = _dot(og_ref[...], w_out_ref[...])
    o_ref[0] = _layer_norm(dn_alpha * x_ref[0] + mix, lg_ref[...], lbias_ref[...])


def _hgrn_layer(x, w_in, lb, norm_g, w_out, ln_g, ln_b, dn_alpha):
    b, s, d = x.shape
    ts = min(SEQ_TILE, s)
    vec = pl.BlockSpec((1, d), lambda bi, si: (0, 0))
    return pl.pallas_call(
        functools.partial(_hgrn_kernel, dn_alpha=dn_alpha),
        grid=(b, s // ts),
        in_specs=[
            pl.BlockSpec((1, ts, d), lambda bi, si: (bi, si, 0)),
            pl.BlockSpec((d, 4 * d), lambda bi, si: (0, 0)),
            vec, vec,
            pl.BlockSpec((d, d), lambda bi, si: (0, 0)),
            vec, vec,
        ],
        out_specs=pl.BlockSpec((1, ts, d), lambda bi, si: (bi, si, 0)),
        out_shape=jax.ShapeDtypeStruct((b, s, d), F32),
        scratch_shapes=[
            pltpu.VMEM((ts, 4 * d), F32),
            pltpu.VMEM((ts, d), F32),
            pltpu.VMEM((ts, d), BF16),
            pltpu.VMEM((d // HGRN_HEAD, HGRN_HEAD, HGRN_HEAD), F32),
            pltpu.VMEM((HGRN_CHUNK, d), F32),
            pltpu.VMEM((HGRN_CHUNK, d), F32),
            pltpu.VMEM((HGRN_CHUNK, d), F32),
        ],
        compiler_params=pltpu.CompilerParams(
            dimension_semantics=("arbitrary", "arbitrary"),
            vmem_limit_bytes=V7X_VMEM_LIMIT),
        name="hgrn_mixer",
    )(x, w_in, lb, norm_g, w_out, ln_g, ln_b)


def _ret_kernel(x_ref, w_in_ref, cos_ref, sin_e_ref, sin_o_ref, w_out_ref, lg_ref, lbias_ref,
                o_ref, h_ref, og_ref, st_ref, *, dn_alpha):
    ts, d = x_ref.shape[1], x_ref.shape[2]
    qk = d
    vd = 2 * d
    hk = qk // RET_HEADS
    hv = vd // RET_HEADS
    c = RET_CHUNK

    @pl.when(pl.program_id(1) == 0)
    def _():
        st_ref[...] = jnp.zeros_like(st_ref)

    h_ref[...] = _dot(x_ref[0].astype(BF16), w_in_ref[...])

    ri = lax.broadcasted_iota(jnp.int32, (c, c), 0)
    cj = lax.broadcasted_iota(jnp.int32, (c, c), 1)
    rel = (ri - cj).astype(F32)
    idx_col = lax.broadcasted_iota(jnp.int32, (c, 1), 0).astype(F32)

    def chunk_body(ci, carry):
        r0 = pl.multiple_of(ci * c, c)
        rows = pl.ds(r0, c)
        cos = jnp.tile(cos_ref[rows, :], (1, RET_HEADS))
        sin_e = jnp.tile(sin_e_ref[rows, :], (1, RET_HEADS))
        sin_o = jnp.tile(sin_o_ref[rows, :], (1, RET_HEADS))

        def rope(t):
            nxt = pltpu.roll(t, qk - 1, 1)
            prv = pltpu.roll(t, 1, 1)
            return t * cos + nxt * sin_e + prv * sin_o

        q = rope(h_ref[rows, 0:qk]).astype(BF16)
        k = rope(h_ref[rows, qk:2 * qk]) * (hk ** -0.5)
        v = h_ref[rows, 2 * qk:2 * qk + vd].astype(BF16)
        lgam = [math.log(1.0 - 2.0 ** (-5.0 - hd)) for hd in range(RET_HEADS)]
        staged = []
        for hd in range(RET_HEADS):
            zeta = jnp.exp((c - 1.0 - idx_col) * lgam[hd])
            q_h = q[:, hk * hd:hk * (hd + 1)]
            k_h = k[:, hk * hd:hk * (hd + 1)]
            v_h = v[:, hv * hd:hv * (hd + 1)]
            st = st_ref[hd]
            sc = _dot(q_h, k_h.astype(BF16), _NT)
            o_inter = _dot(q_h, st.astype(BF16))
            new_st = st * math.exp(c * lgam[hd]) + _dot((k_h * zeta).astype(BF16), v_h, _TN)
            staged.append((sc, o_inter, new_st, v_h))
        for hd, (sc, o_inter, new_st, v_h) in enumerate(staged):
            decay = jnp.where(rel >= 0, jnp.exp(rel * lgam[hd]), 0.0)
            xi = jnp.exp((idx_col + 1.0) * lgam[hd])
            o_h = _dot((sc * decay).astype(BF16), v_h) + o_inter * xi
            st_ref[hd] = new_st
            mu = jnp.mean(o_h, axis=-1, keepdims=True)
            dlt = o_h - mu
            var = jnp.mean(dlt * dlt, axis=-1, keepdims=True)
            g_h = h_ref[rows, 2 * qk + vd + hv * hd:2 * qk + vd + hv * (hd + 1)]
            og_ref[rows, hv * hd:hv * (hd + 1)] = (
                (g_h * jax.nn.sigmoid(g_h)) * (dlt * lax.rsqrt(var + LN_EPS))).astype(BF16)
        return carry

    lax.fori_loop(0, ts // c, chunk_body, 0)

    mix = _dot(og_ref[...], w_out_ref[...])
    o_ref[0] = _layer_norm(dn_alpha * x_ref[0] + mix, lg_ref[...], lbias_ref[...])


def _ret_layer(x, w_in, cos, sin_e, sin_o, w_out, ln_g, ln_b, dn_alpha):
    b, s, d = x.shape
    ts = min(SEQ_TILE, s)
    n_in = w_in.shape[1]
    hk = d // RET_HEADS
    vec = pl.BlockSpec((1, d), lambda bi, si: (0, 0))
    tab = pl.BlockSpec((ts, hk), lambda bi, si: (si, 0))
    return pl.pallas_call(
        functools.partial(_ret_kernel, dn_alpha=dn_alpha),
        grid=(b, s // ts),
        in_specs=[
            pl.BlockSpec((1, ts, d), lambda bi, si: (bi, si, 0)),
            pl.BlockSpec((d, n_in), lambda bi, si: (0, 0)),
            tab, tab, tab,
            pl.BlockSpec((2 * d, d), lambda bi, si: (0, 0)),
            vec, vec,
        ],
        out_specs=pl.BlockSpec((1, ts, d), lambda bi, si: (bi, si, 0)),
        out_shape=jax.ShapeDtypeStruct((b, s, d), F32),
        scratch_shapes=[
            pltpu.VMEM((ts, n_in), F32),
            pltpu.VMEM((ts, 2 * d), BF16),
            pltpu.VMEM((RET_HEADS, hk, 2 * d // RET_HEADS), F32),
        ],
        compiler_params=pltpu.CompilerParams(
            dimension_semantics=("arbitrary", "arbitrary"),
            vmem_limit_bytes=V7X_VMEM_LIMIT),
        name="ret_mixer",
    )(x, w_in, cos, sin_e, sin_o, w_out, ln_g, ln_b)


def _rotary_tables(seq_len, hk):
    pos = jnp.arange(seq_len, dtype=F32)
    inv = 1.0 / (ROPE_BASE ** jnp.linspace(0.0, 1.0, hk // 2, dtype=F32))
    theta = pos[:, None] * jnp.repeat(inv, 2)[None, :]
    cos, sin = jnp.cos(theta), jnp.sin(theta)
    even = (jnp.arange(hk) % 2 == 0)[None, :]
    return cos, jnp.where(even, -sin, 0.0), jnp.where(even, 0.0, sin)


def _router_kernel(x_ref, w_ref, b_ref, idx_ref, gate_ref, rank_ref, cnt_ref, acc_ref):
    t = x_ref.shape[0]
    e = w_ref.shape[0]

    @pl.when(pl.program_id(0) == 0)
    def _():
        acc_ref[...] = jnp.zeros_like(acc_ref)

    xh, xm, xl = _split3(x_ref[...])
    wh, wm, wl = _split3(w_ref[...])
    logits = (_dot(wl, xh, _NT) + _dot(wh, xl, _NT) + _dot(wm, xm, _NT)
              + _dot(wm, xh, _NT) + _dot(wh, xm, _NT) + _dot(wh, xh, _NT)) + b_ref[...]

    eid = lax.broadcasted_iota(jnp.int32, (e, t), 0)
    ti = lax.broadcasted_iota(jnp.int32, (t, t), 0)
    tj = lax.broadcasted_iota(jnp.int32, (t, t), 1)
    before = jnp.where(ti < tj, 1.0, 0.0).astype(BF16)
    work = logits
    vals, idxs, ranks = [], [], []
    base = acc_ref[...]
    for _ in range(TOP_K):
        m = jnp.max(work, axis=0, keepdims=True)
        sel = jnp.min(jnp.where(work == m, eid, e), axis=0, keepdims=True)
        hit = eid == sel
        work = jnp.where(hit, -jnp.inf, work)
        onehot = jnp.where(hit, 1.0, 0.0)
        excl = _dot(onehot.astype(BF16), before)
        rank = jnp.sum(onehot * (excl + base), axis=0, keepdims=True)
        base = base + jnp.sum(onehot, axis=1, keepdims=True)
        vals.append(m)
        idxs.append(sel)
        ranks.append(rank)
    acc_ref[...] = base
    cnt_ref[...] = base.astype(jnp.int32)
    ex = [jnp.exp(vv - vals[0]) for vv in vals]
    den = ex[0] + ex[1] + ex[2] + ex[3]
    gate_ref[...] = jnp.concatenate(ex, axis=0) / den
    idx_ref[...] = jnp.concatenate(idxs, axis=0)
    rank_ref[...] = jnp.concatenate(ranks, axis=0).astype(jnp.int32)


def _router(x2, w_t, b_col):
    n, d = x2.shape
    e = w_t.shape[0]
    t = min(ROUTER_TILE, n)
    kt = pl.BlockSpec((TOP_K, t), lambda i: (0, i))
    return pl.pallas_call(
        _router_kernel,
        grid=(n // t,),
        in_specs=[
            pl.BlockSpec((t, d), lambda i: (i, 0)),
            pl.BlockSpec((e, d), lambda i: (0, 0)),
            pl.BlockSpec((e, 1), lambda i: (0, 0)),
        ],
        out_specs=[kt, kt, kt, pl.BlockSpec((e, 1), lambda i: (0, 0))],
        out_shape=[
            jax.ShapeDtypeStruct((TOP_K, n), jnp.int32),
            jax.ShapeDtypeStruct((TOP_K, n), F32),
            jax.ShapeDtypeStruct((TOP_K, n), jnp.int32),
            jax.ShapeDtypeStruct((e, 1), jnp.int32),
        ],
        scratch_shapes=[pltpu.VMEM((e, 1), F32)],
        compiler_params=pltpu.CompilerParams(dimension_semantics=("arbitrary",)),
        name="moe_router",
    )(x2, w_t, b_col)


def _plan_kernel(cnt_ref, idx_ref, rank_ref, dest_ref, blk_ref, fill_ref, *, move_tile):
    e = cnt_ref.shape[0]
    pt = idx_ref.shape[1]
    shift = EXPERT_ROWS.bit_length() - 1
    cnt_col = cnt_ref[...]
    pad_col = (((cnt_col + (EXPERT_ROWS - 1)) >> shift) << shift).astype(F32)
    r = lax.broadcasted_iota(jnp.int32, (e, e), 0)
    c = lax.broadcasted_iota(jnp.int32, (e, e), 1)
    pad_row = jnp.sum(jnp.where(r == c, pad_col, 0.0), axis=0, keepdims=True)
    start_col = jnp.sum(jnp.where(c < r, pad_row, 0.0), axis=1, keepdims=True)
    end_col = start_col + pad_col
    used_col = start_col + cnt_col.astype(F32)

    nb = blk_ref.shape[1]
    starts = (lax.broadcasted_iota(jnp.int32, (e, nb), 1) * EXPERT_ROWS).astype(F32)
    owner = jnp.sum(jnp.where(end_col <= starts, 1.0, 0.0), axis=0, keepdims=True)
    blk_ref[...] = jnp.minimum(owner, e - 1.0).astype(jnp.int32)

    lane = lax.broadcasted_iota(jnp.int32, (e, fill_ref.shape[1]), 1)
    sub = lax.broadcasted_iota(jnp.int32, (e, fill_ref.shape[1]), 0)
    fill = jnp.where(lane == sub, used_col, 0.0) + jnp.where(lane == sub + e, end_col, 0.0)
    fill_ref[...] = jnp.sum(fill, axis=0, keepdims=True).astype(jnp.int32)

    eid = lax.broadcasted_iota(jnp.int32, (e, pt), 0)
    for kk in range(TOP_K):
        hit = eid == idx_ref[kk:kk + 1, :]
        base = jnp.sum(jnp.where(hit, start_col, 0.0), axis=0, keepdims=True)
        d_k = base.astype(jnp.int32) + rank_ref[kk:kk + 1, :]
        for j in range(pt // move_tile):
            dest_ref[j, :, kk * move_tile:(kk + 1) * move_tile] = d_k[:, j * move_tile:(j + 1) * move_tile]


def _plan(counts, idx, rank, n_blocks, move_tile):
    e = counts.shape[0]
    n = idx.shape[1]
    pt = min(PLAN_TILE, n)
    kt = pl.BlockSpec((TOP_K, pt), lambda i: (0, i))
    fill_lanes = 128 * (-(-2 * e // 128))
    return pl.pallas_call(
        functools.partial(_plan_kernel, move_tile=move_tile),
        grid=(n // pt,),
        in_specs=[pl.BlockSpec((e, 1), lambda i: (0, 0)), kt, kt],
        out_specs=[
            pl.BlockSpec((pt // move_tile, 1, TOP_K * move_tile), lambda i: (i, 0, 0)),
            pl.BlockSpec((1, n_blocks), lambda i: (0, 0)),
            pl.BlockSpec((1, fill_lanes), lambda i: (0, 0)),
        ],
        out_shape=[
            jax.ShapeDtypeStruct((n // move_tile, 1, TOP_K * move_tile), jnp.int32),
            jax.ShapeDtypeStruct((1, n_blocks), jnp.int32),
            jax.ShapeDtypeStruct((1, fill_lanes), jnp.int32),
        ],
        compiler_params=pltpu.CompilerParams(dimension_semantics=("arbitrary",)),
        name="moe_plan",
    )(counts, idx, rank)


def _row_copy(src_ref, src_row, dst_ref, dst_row, sem):
    return pltpu.make_async_copy(src_ref.at[pl.ds(src_row, 1)], dst_ref.at[pl.ds(dst_row, 1)], sem)


def _fetch_move_indices(dest_hbm, dest_smem, sem_idx):
    i = pl.program_id(0)
    slot = i % 2

    def copy(step, s):
        return pltpu.make_async_copy(dest_hbm.at[step, 0], dest_smem.at[s], sem_idx.at[s])

    @pl.when(i == 0)
    def _():
        copy(0, 0).start()

    @pl.when(i + 1 < pl.num_programs(0))
    def _():
        copy(i + 1, 1 - slot).start()

    copy(i, slot).wait()
    return slot


def _dispatch_kernel(fill_ref, dest_hbm, x_ref, xs_out, dest_smem, zero_ref, sem_idx, sem_rows, sem_zero,
                     *, n_experts):
    t = x_ref.shape[0]
    n_rows = xs_out.shape[0]
    i = pl.program_id(0)

    @pl.when(i == 0)
    def _():
        zero_ref[...] = jnp.zeros_like(zero_ref)
        tail_blk = fill_ref[2 * n_experts - 1] // EXPERT_ROWS

        def blk_copy(bi):
            r0 = pl.multiple_of(bi * EXPERT_ROWS, EXPERT_ROWS)
            return pltpu.make_async_copy(zero_ref, xs_out.at[pl.ds(r0, EXPERT_ROWS)], sem_zero)

        def pad_rows(fn):
            def per_expert(ee, carry):
                def one(rr, c2):
                    fn(_row_copy(zero_ref, 0, xs_out, rr, sem_zero))
                    return c2
                lax.fori_loop(fill_ref[ee], fill_ref[n_experts + ee], one, 0)
                return carry
            lax.fori_loop(0, n_experts, per_expert, 0)

            def per_tail(bi, carry):
                fn(blk_copy(bi))
                return carry
            lax.fori_loop(tail_blk, n_rows // EXPERT_ROWS, per_tail, 0)

        pad_rows(lambda cp: cp.start())
        pad_rows(lambda cp: cp.wait())

    slot = _fetch_move_indices(dest_hbm, dest_smem, sem_idx)

    def issue(tt, carry):
        for kk in range(TOP_K):
            _row_copy(x_ref, tt, xs_out, dest_smem[slot, kk * t + tt], sem_rows).start()
        return carry

    lax.fori_loop(0, t, issue, 0, unroll=8)

    def drain(tt, carry):
        for kk in range(TOP_K):
            _row_copy(x_ref, 0, xs_out, 0, sem_rows).wait()
        return carry

    lax.fori_loop(0, t, drain, 0, unroll=8)


def _dispatch(x2, dest_blocks, fill, n_rows, n_experts):
    n, d = x2.shape
    t = dest_blocks.shape[2] // TOP_K
    grid_spec = pltpu.PrefetchScalarGridSpec(
        num_scalar_prefetch=1,
        grid=(n // t,),
        in_specs=[
            pl.BlockSpec(memory_space=pl.ANY),
            pl.BlockSpec((t, d), lambda i, f: (i, 0)),
        ],
        out_specs=pl.BlockSpec(memory_space=pl.ANY),
        scratch_shapes=[
            pltpu.SMEM((2, TOP_K * t), jnp.int32),
            pltpu.VMEM((EXPERT_ROWS, d), x2.dtype),
            pltpu.SemaphoreType.DMA((2,)),
            pltpu.SemaphoreType.DMA,
            pltpu.SemaphoreType.DMA,
        ],
    )
    return pl.pallas_call(
        functools.partial(_dispatch_kernel, n_experts=n_experts),
        grid_spec=grid_spec,
        out_shape=jax.ShapeDtypeStruct((n_rows, d), x2.dtype),
        compiler_params=pltpu.CompilerParams(dimension_semantics=("arbitrary",)),
        name="moe_dispatch",
    )(fill, dest_blocks, x2)


def _expert_kernel(blk_e_ref, fill_ref, xs_ref, wgu_ref, bg_ref, bl_ref, wd_ref, bd_ref, y_ref,
                   wt_ref, wg_s, wl_s, wd_s, *, n_experts):
    i = pl.program_id(0)
    ff, d = wg_s.shape
    tc, rk = wt_ref.shape
    half = tc // 2
    changed = blk_e_ref[i] != blk_e_ref[jnp.maximum(i - 1, 0)]
    used = i * EXPERT_ROWS < fill_ref[2 * n_experts - 1]

    @pl.when((i == 0) | changed)
    def _():
        for kb in range(d // rk):
            for j in range(2 * ff // tc):
                wt_ref[...] = wgu_ref[0, 0, rk * kb:rk * (kb + 1), tc * j:tc * (j + 1)].T
                wg_s[half * j:half * (j + 1), rk * kb:rk * (kb + 1)] = (
                    wt_ref[pl.ds(0, half, stride=2), :].astype(BF16))
                wl_s[half * j:half * (j + 1), rk * kb:rk * (kb + 1)] = (
                    wt_ref[pl.ds(1, half, stride=2), :].astype(BF16))
        wd_s[...] = wd_ref[0, 0].astype(BF16)

    @pl.when(used)
    def _():
        xb = xs_ref[...].astype(BF16)
        hg = _dot(xb, wg_s[...], _NT) + bg_ref[0]
        hl = _dot(xb, wl_s[...], _NT) + bl_ref[0]
        glu = jnp.minimum(hg, SWIGLU_LIMIT)
        lin = jnp.clip(hl, -SWIGLU_LIMIT, SWIGLU_LIMIT)
        a = glu * jax.nn.sigmoid(SWIGLU_ALPHA * glu) * (lin + 1.0)
        y_ref[...] = _dot(a.astype(BF16), wd_s[...]) + bd_ref[0, 0]

    @pl.when(jnp.logical_not(used))
    def _():
        y_ref[...] = jnp.zeros_like(y_ref)


def _experts(xs, blk_e, fill, w_gu, b_glu, b_lin, w_dn, b_dn, layer):
    n_rows, d = xs.shape
    ff = w_dn.shape[2]
    e = w_dn.shape[1]
    bm = EXPERT_ROWS
    wspec = lambda r, c: pl.BlockSpec((1, r, c), lambda i, be, fl: (be[i], 0, 0))
    lspec = lambda r, c: pl.BlockSpec((1, 1, r, c), lambda i, be, fl: (layer, be[i], 0, 0))
    grid_spec = pltpu.PrefetchScalarGridSpec(
        num_scalar_prefetch=2,
        grid=(n_rows // bm,),
        in_specs=[
            pl.BlockSpec((bm, d), lambda i, be, fl: (i, 0)),
            lspec(d, 2 * ff), wspec(1, ff), wspec(1, ff),
            lspec(ff, d), lspec(1, d),
        ],
        out_specs=pl.BlockSpec((bm, d), lambda i, be, fl: (i, 0)),
        scratch_shapes=[
            pltpu.VMEM((WEIGHT_SPLIT_COLS, V7X_LANES), F32),
            pltpu.VMEM((ff, d), BF16),
            pltpu.VMEM((ff, d), BF16),
            pltpu.VMEM((ff, d), BF16),
        ],
    )
    return pl.pallas_call(
        functools.partial(_expert_kernel, n_experts=e),
        grid_spec=grid_spec,
        out_shape=jax.ShapeDtypeStruct((n_rows, d), F32),
        compiler_params=pltpu.CompilerParams(
            dimension_semantics=("arbitrary",), vmem_limit_bytes=V7X_VMEM_LIMIT),
        name="moe_experts",
    )(blk_e, fill, xs, w_gu, b_glu, b_lin, w_dn, b_dn)


def _combine_kernel(dest_hbm, y_hbm, x_ref, gate_ref, lg_ref, lbias_ref, o_ref,
                    dest_smem, rows_ref, sem_idx, sem_rows, *, dn_alpha):
    t = x_ref.shape[0]
    slot = _fetch_move_indices(dest_hbm, dest_smem, sem_idx)

    def issue(tt, carry):
        for kk in range(TOP_K):
            _row_copy(y_hbm, dest_smem[slot, kk * t + tt], rows_ref.at[kk], tt, sem_rows).start()
        return carry

    lax.fori_loop(0, t, issue, 0, unroll=8)

    def drain(tt, carry):
        for kk in range(TOP_K):
            _row_copy(y_hbm, 0, rows_ref.at[kk], 0, sem_rows).wait()
        return carry

    lax.fori_loop(0, t, drain, 0, unroll=8)

    gate = gate_ref[...]
    ffn = rows_ref[0] * gate[:, 0:1]
    for kk in range(1, TOP_K):
        ffn = ffn + rows_ref[kk] * gate[:, kk:kk + 1]
    o_ref[...] = _layer_norm(dn_alpha * x_ref[...] + ffn, lg_ref[...], lbias_ref[...])


def _combine(y, x2, dest_blocks, gate_t, ln_g, ln_b, dn_alpha):
    n, d = x2.shape
    t = dest_blocks.shape[2] // TOP_K
    vec = pl.BlockSpec((1, d), lambda i: (0, 0))
    return pl.pallas_call(
        functools.partial(_combine_kernel, dn_alpha=dn_alpha),
        grid=(n // t,),
        in_specs=[
            pl.BlockSpec(memory_space=pl.ANY),
            pl.BlockSpec(memory_space=pl.ANY),
            pl.BlockSpec((t, d), lambda i: (i, 0)),
            pl.BlockSpec((t, TOP_K), lambda i: (i, 0)),
            vec, vec,
        ],
        out_specs=pl.BlockSpec((t, d), lambda i: (i, 0)),
        out_shape=jax.ShapeDtypeStruct((n, d), F32),
        scratch_shapes=[
            pltpu.SMEM((2, TOP_K * t), jnp.int32),
            pltpu.VMEM((TOP_K, t, d), F32),
            pltpu.SemaphoreType.DMA((2,)),
            pltpu.SemaphoreType.DMA,
        ],
        compiler_params=pltpu.CompilerParams(dimension_semantics=("arbitrary",)),
        name="moe_combine",
    )(dest_blocks, y, x2, gate_t, ln_g, ln_b)


def _moe_layer(x2, w_r, b_r, w_gu, b_glu, b_lin, w_dn, b_dn, layer, ln_g, ln_b, dn_alpha):
    n, d = x2.shape
    e = w_r.shape[1]
    idx, gate, rank, counts = _router(x2, w_r.T, b_r.reshape(e, 1))
    n_blocks = -(-(n * TOP_K) // EXPERT_ROWS) + e
    dest_blocks, blk_e, fill = _plan(counts, idx, rank, n_blocks, min(MOVE_TILE, n))
    fill = fill.reshape(-1)
    xs = _dispatch(x2, dest_blocks, fill, n_blocks * EXPERT_ROWS, e)
    y = _experts(xs, blk_e.reshape(-1), fill, w_gu, b_glu, b_lin, w_dn, b_dn, layer)
    return _combine(y, x2, dest_blocks, gate.T, ln_g, ln_b, dn_alpha)


def kernel(x, hgrn_w_in, hgrn_norm_g, hgrn_w_out, lower_bounds, ret_w_in, ret_w_out, ln_mix_g, ln_mix_b,
           ln_ffn_g, ln_ffn_b, router_w, router_b, expert_w_gu, expert_b_gu, expert_w_down, expert_b_down):
    b, s, d = x.shape
    depth = lower_bounds.shape[0]
    dn_alpha = (2 * depth) ** 0.25
    lb_soft = jax.nn.softmax(lower_bounds.astype(F32), axis=0)
    lbs = jnp.cumsum(lb_soft, axis=0) - lb_soft[0]
    cos, sin_e, sin_o = _rotary_tables(s, d // RET_HEADS)
    row = lambda p, i: p[i].reshape(1, -1)
    for i in range(depth):
        j = i // 2
        if i % 2 == 0:
            x = _hgrn_layer(x, hgrn_w_in[j].astype(BF16), row(lbs, i), row(hgrn_norm_g, j),
                            hgrn_w_out[j].astype(BF16), row(ln_mix_g, i), row(ln_mix_b, i), dn_alpha)
        else:
            x = _ret_layer(x, ret_w_in[j].astype(BF16), cos, sin_e, sin_o, ret_w_out[j].astype(BF16),
                           row(ln_mix_g, i), row(ln_mix_b, i), dn_alpha)
        b_gu = expert_b_gu[i]
        x2 = _moe_layer(
            x.reshape(b * s, d), router_w[i], router_b[i], expert_w_gu,
            b_gu[:, None, 0::2], b_gu[:, None, 1::2],
            expert_w_down, expert_b_down[:, :, None, :], i,
            row(ln_ffn_g, i), row(ln_ffn_b, i), dn_alpha)
        x = x2.reshape(b, s, d)
    return x
```

```python
import functools
import math

import jax
import jax.numpy as jnp
from jax import lax
from jax.experimental import pallas as pl
from jax.experimental.pallas import tpu as pltpu

HGRN_HEAD = 128
RET_HEADS = 4
ROPE_BASE = 10000.0
N_EXPERTS = 32
TOP_K = 4
SWIGLU_LIMIT = 7.0
SWIGLU_ALPHA = 1.702
LN_EPS = 1e-5
RMS_EPS = 1e-6

V7X_VMEM_LIMIT = 56 * 1024 * 1024
V7X_LANES = 128
SEQ_TILE = 256
HGRN_CHUNK = 64
HGRN_SUB = 16
RET_CHUNK = 128
ROUTER_TILE = 512
EXPERT_ROWS = 512
ROW_GRANULE = 8
SLAB_BITS = (ROUTER_TILE // ROW_GRANULE).bit_length()
SORT_CHUNK = 256
WEIGHT_SPLIT_COLS = 256

F32 = jnp.float32
BF16 = jnp.bfloat16

_NT = (((1,), (1,)), ((), ()))
_TN = (((0,), (0,)), ((), ()))


def _dot(a, b, dims=None):
    if dims is None:
        return jnp.dot(a, b, preferred_element_type=F32)
    return lax.dot_general(a, b, dims, preferred_element_type=F32)


def _layer_norm(v, g, b):
    mu = jnp.mean(v, axis=-1, keepdims=True)
    d = v - mu
    var = jnp.mean(d * d, axis=-1, keepdims=True)
    return d * lax.rsqrt(var + LN_EPS) * g + b


def _split3(v):
    hi = v.astype(BF16)
    r1 = v - hi.astype(F32)
    mid = r1.astype(BF16)
    lo = (r1 - mid.astype(F32)).astype(BF16)
    return hi, mid, lo


def _hgrn_kernel(x_ref, w_in_ref, lb_ref, ng_ref, w_out_ref, lg_ref, lbias_ref,
                 o_ref, h_ref, oacc_ref, og_ref, st_ref, qa_ref, k_ref, b_ref, *, dn_alpha):
    ts, d = x_ref.shape[1], x_ref.shape[2]
    n_heads = d // HGRN_HEAD
    c = HGRN_CHUNK
    n_sub = c // HGRN_SUB

    @pl.when(pl.program_id(1) == 0)
    def _():
        st_ref[...] = jnp.zeros_like(st_ref)

    h_ref[...] = _dot(x_ref[0].astype(BF16), w_in_ref[...])

    sub = HGRN_SUB
    row = lax.broadcasted_iota(jnp.int32, (c, c), 0)
    col = lax.broadcasted_iota(jnp.int32, (c, c), 1)
    sub_shift = sub.bit_length() - 1
    same_sub = (row >> sub_shift) == (col >> sub_shift)
    tril = jnp.where(same_sub & (row >= col), 1.0, 0.0).astype(BF16)

    def gates(rows):
        lb = lb_ref[...]
        one_m_lb = 1.0 - lb
        q = h_ref[rows, 0:d]
        sf = h_ref[rows, d:2 * d]
        qa_ref[...] = q * jax.nn.sigmoid(q)
        k_ref[...] = one_m_lb * jax.nn.sigmoid(-sf)
        logf = jnp.log(lb + one_m_lb * jax.nn.sigmoid(sf))
        hi = logf.astype(BF16)
        lo = (logf - hi.astype(F32)).astype(BF16)
        b_ref[...] = _dot(tril, lo) + _dot(tril, hi)

    def head_scores(rows, hd):
        lo_, hi_ = HGRN_HEAD * hd, HGRN_HEAD * (hd + 1)
        b = b_ref[:, lo_:hi_]
        vb = h_ref[rows, 2 * d + lo_:2 * d + hi_].astype(BF16)
        tot = [b[sub * j + sub - 1:sub * j + sub, :] for j in range(n_sub)]
        to_end = jnp.concatenate([jnp.broadcast_to(t, (sub, HGRN_HEAD)) for t in tot], axis=0) - b
        q_dec = qa_ref[:, lo_:hi_] * jnp.exp(b)
        k_st = k_ref[:, lo_:hi_] * jnp.exp(to_end)
        g_start = [jnp.zeros_like(tot[0])]
        for j in range(n_sub):
            g_start.append(g_start[j] + tot[j])
        g_last = g_start[n_sub]
        qd = [q_dec[sub * j:sub * (j + 1), :] for j in range(n_sub)]
        ks = [k_st[sub * j:sub * (j + 1), :] for j in range(n_sub)]
        q_in = jnp.concatenate([qd[j] * jnp.exp(g_start[j]) for j in range(n_sub)], axis=0).astype(BF16)
        k_out = jnp.concatenate([ks[j] * jnp.exp(g_last - g_start[j + 1]) for j in range(n_sub)],
                                axis=0).astype(BF16)
        st = st_ref[hd]
        o_inter = _dot(q_in, st.astype(BF16), _NT)
        scs = []
        for i in range(n_sub):
            qm = jnp.concatenate([qd[j] * jnp.exp(g_start[j] - g_start[i + 1]) for j in range(i, n_sub)],
                                 axis=0).astype(BF16)
            scs.append(_dot(qm, ks[i].astype(BF16), _NT))
        new_st = st * jnp.exp(g_last) + _dot(vb, k_out, _TN)
        return o_inter, scs, vb, new_st

    def head_output(rows, hd, o_h, scs, vb, new_st):
        lo_, hi_ = HGRN_HEAD * hd, HGRN_HEAD * (hd + 1)
        for i, sc in enumerate(scs):
            rr = lax.broadcasted_iota(jnp.int32, sc.shape, 0)
            cc = lax.broadcasted_iota(jnp.int32, sc.shape, 1)
            sc = jnp.where(rr >= cc, sc, 0.0).astype(BF16)
            part = _dot(sc, vb[sub * i:sub * (i + 1), :])
            if i:
                part = jnp.concatenate([jnp.zeros((sub * i, HGRN_HEAD), F32), part], axis=0)
            o_h = o_h + part
        st_ref[hd] = new_st
        oacc_ref[rows, lo_:hi_] = o_h

    strip = 16

    def chunk_body(ci, carry):
        r0 = pl.multiple_of(ci * c, c)
        rows = pl.ds(r0, c)
        gates(rows)
        staged = [head_scores(rows, hd) for hd in range(n_heads)]
        for hd in range(n_heads):
            head_output(rows, hd, *staged[hd])
        for si in range(c // strip):
            rs = pl.ds(pl.multiple_of(r0 + si * strip, strip), strip)
            o = oacc_ref[rs, :]
            g = h_ref[rs, 3 * d:4 * d]
            o = o * lax.rsqrt(jnp.mean(o * o, axis=-1, keepdims=True) + RMS_EPS) * ng_ref[...]
            og_ref[rs, :] = (o * (g * jax.nn.sigmoid(g))).astype(BF16)
        return carry

    lax.fori_loop(0, ts // c, chunk_body, 0)

    mix = _dot(og_ref[...], w_out_ref[...])
    o_ref[0] = _layer_norm(dn_alpha * x_ref[0] + mix, lg_ref[...], lbias_ref[...])


def _hgrn_layer(x, w_in, lb, norm_g, w_out, ln_g, ln_b, dn_alpha):
    b, s, d = x.shape
    ts = min(SEQ_TILE, s)
    vec = pl.BlockSpec((1, d), lambda bi, si: (0, 0))
    return pl.pallas_call(
        functools.partial(_hgrn_kernel, dn_alpha=dn_alpha),
        grid=(b, s // ts),
        in_specs=[
            pl.BlockSpec((1, ts, d), lambda bi, si: (bi, si, 0)),
            pl.BlockSpec((d, 4 * d), lambda bi, si: (0, 0)),
            vec, vec,
            pl.BlockSpec((d, d), lambda bi, si: (0, 0)),
            vec, vec,
        ],
        out_specs=pl.BlockSpec((1, ts, d), lambda bi, si: (bi, si, 0)),
        out_shape=jax.ShapeDtypeStruct((b, s, d), F32),
        scratch_shapes=[
            pltpu.VMEM((ts, 4 * d), F32),
            pltpu.VMEM((ts, d), F32),
            pltpu.VMEM((ts, d), BF16),
            pltpu.VMEM((d // HGRN_HEAD, HGRN_HEAD, HGRN_HEAD), F32),
            pltpu.VMEM((HGRN_CHUNK, d), F32),
            pltpu.VMEM((HGRN_CHUNK, d), F32),
            pltpu.VMEM((HGRN_CHUNK, d), F32),
        ],
        compiler_params=pltpu.CompilerParams(
            dimension_semantics=("arbitrary", "arbitrary"),
            vmem_limit_bytes=V7X_VMEM_LIMIT),
        name="hgrn_mixer",
    )(x, w_in, lb, norm_g, w_out, ln_g, ln_b)


def _ret_kernel(x_ref, w_in_ref, cos_ref, sin_e_ref, sin_o_ref, w_out_ref, lg_ref, lbias_ref,
                o_ref, h_ref, og_ref, st_ref, *, dn_alpha):
    ts, d = x_ref.shape[1], x_ref.shape[2]
    qk = d
    vd = 2 * d
    hk = qk // RET_HEADS
    hv = vd // RET_HEADS
    c = RET_CHUNK

    @pl.when(pl.program_id(1) == 0)
    def _():
        st_ref[...] = jnp.zeros_like(st_ref)

    h_ref[...] = _dot(x_ref[0].astype(BF16), w_in_ref[...])

    ri = lax.broadcasted_iota(jnp.int32, (c, c), 0)
    cj = lax.broadcasted_iota(jnp.int32, (c, c), 1)
    rel = (ri - cj).astype(F32)
    idx_col = lax.broadcasted_iota(jnp.int32, (c, 1), 0).astype(F32)

    def chunk_body(ci, carry):
        r0 = pl.multiple_of(ci * c, c)
        rows = pl.ds(r0, c)
        cos = jnp.tile(cos_ref[rows, :], (1, RET_HEADS))
        sin_e = jnp.tile(sin_e_ref[rows, :], (1, RET_HEADS))
        sin_o = jnp.tile(sin_o_ref[rows, :], (1, RET_HEADS))

        def rope(t):
            nxt = pltpu.roll(t, qk - 1, 1)
            prv = pltpu.roll(t, 1, 1)
            return t * cos + nxt * sin_e + prv * sin_o

        q = rope(h_ref[rows, 0:qk]).astype(BF16)
        k = rope(h_ref[rows, qk:2 * qk]) * (hk ** -0.5)
        v = h_ref[rows, 2 * qk:2 * qk + vd].astype(BF16)
        lgam = [math.log(1.0 - 2.0 ** (-5.0 - hd)) for hd in range(RET_HEADS)]
        staged = []
        for hd in range(RET_HEADS):
            zeta = jnp.exp((c - 1.0 - idx_col) * lgam[hd])
            q_h = q[:, hk * hd:hk * (hd + 1)]
            k_h = k[:, hk * hd:hk * (hd + 1)]
            v_h = v[:, hv * hd:hv * (hd + 1)]
            st = st_ref[hd]
            sc = _dot(q_h, k_h.astype(BF16), _NT)
            o_inter = _dot(q_h, st.astype(BF16))
            new_st = st * math.exp(c * lgam[hd]) + _dot((k_h * zeta).astype(BF16), v_h, _TN)
            staged.append((sc, o_inter, new_st, v_h))
        for hd, (sc, o_inter, new_st, v_h) in enumerate(staged):
            decay = jnp.where(rel >= 0, jnp.exp(rel * lgam[hd]), 0.0)
            xi = jnp.exp((idx_col + 1.0) * lgam[hd])
            o_h = _dot((sc * decay).astype(BF16), v_h) + o_inter * xi
            st_ref[hd] = new_st
            mu = jnp.mean(o_h, axis=-1, keepdims=True)
            dlt = o_h - mu
            var = jnp.mean(dlt * dlt, axis=-1, keepdims=True)
            g_h = h_ref[rows, 2 * qk + vd + hv * hd:2 * qk + vd + hv * (hd + 1)]
            og_ref[rows, hv * hd:hv * (hd + 1)] = (
                (g_h * jax.nn.sigmoid(g_h)) * (dlt * lax.rsqrt(var + LN_EPS))).astype(BF16)
        return carry

    lax.fori_loop(0, ts // c, chunk_body, 0)

    mix = _dot(og_ref[...], w_out_ref[...])
    o_ref[0] = _layer_norm(dn_alpha * x_ref[0] + mix, lg_ref[...], lbias_ref[...])


def _ret_layer(x, w_in, cos, sin_e, sin_o, w_out, ln_g, ln_b, dn_alpha):
    b, s, d = x.shape
    ts = min(SEQ_TILE, s)
    n_in = w_in.shape[1]
    hk = d // RET_HEADS
    vec = pl.BlockSpec((1, d), lambda bi, si: (0, 0))
    tab = pl.BlockSpec((ts, hk), lambda bi, si: (si, 0))
    return pl.pallas_call(
        functools.partial(_ret_kernel, dn_alpha=dn_alpha),
        grid=(b, s // ts),
        in_specs=[
            pl.BlockSpec((1, ts, d), lambda bi, si: (bi, si, 0)),
            pl.BlockSpec((d, n_in), lambda bi, si: (0, 0)),
            tab, tab, tab,
            pl.BlockSpec((2 * d, d), lambda bi, si: (0, 0)),
            vec, vec,
        ],
        out_specs=pl.BlockSpec((1, ts, d), lambda bi, si: (bi, si, 0)),
        out_shape=jax.ShapeDtypeStruct((b, s, d), F32),
        scratch_shapes=[
            pltpu.VMEM((ts, n_in), F32),
            pltpu.VMEM((ts, 2 * d), BF16),
            pltpu.VMEM((RET_HEADS, hk, 2 * d // RET_HEADS), F32),
        ],
        compiler_params=pltpu.CompilerParams(
            dimension_semantics=("arbitrary", "arbitrary"),
            vmem_limit_bytes=V7X_VMEM_LIMIT),
        name="ret_mixer",
    )(x, w_in, cos, sin_e, sin_o, w_out, ln_g, ln_b)


def _rotary_tables(seq_len, hk):
    pos = jnp.arange(seq_len, dtype=F32)
    inv = 1.0 / (ROPE_BASE ** jnp.linspace(0.0, 1.0, hk // 2, dtype=F32))
    theta = pos[:, None] * jnp.repeat(inv, 2)[None, :]
    cos, sin = jnp.cos(theta), jnp.sin(theta)
    even = (jnp.arange(hk) % 2 == 0)[None, :]
    return cos, jnp.where(even, -sin, 0.0), jnp.where(even, 0.0, sin)


def _router_kernel(x_ref, w_ref, b_ref, idx_ref, gate_ref, rank_ref, cnt_ref):
    t = x_ref.shape[0]
    e = w_ref.shape[0]

    xh, xm, xl = _split3(x_ref[...])
    wh, wm, wl = _split3(w_ref[...])
    logits = (_dot(wl, xh, _NT) + _dot(wh, xl, _NT) + _dot(wm, xm, _NT)
              + _dot(wm, xh, _NT) + _dot(wh, xm, _NT) + _dot(wh, xh, _NT)) + b_ref[...]

    eid = lax.broadcasted_iota(jnp.int32, (e, t), 0)
    ti = lax.broadcasted_iota(jnp.int32, (t, t), 0)
    tj = lax.broadcasted_iota(jnp.int32, (t, t), 1)
    before = jnp.where(ti < tj, 1.0, 0.0).astype(BF16)
    work = logits
    vals, idxs, ranks = [], [], []
    base = jnp.zeros((e, 1), F32)
    for _ in range(TOP_K):
        m = jnp.max(work, axis=0, keepdims=True)
        sel = jnp.min(jnp.where(work == m, eid, e), axis=0, keepdims=True)
        hit = eid == sel
        work = jnp.where(hit, -jnp.inf, work)
        onehot = jnp.where(hit, 1.0, 0.0)
        excl = _dot(onehot.astype(BF16), before)
        rank = jnp.sum(onehot * (excl + base), axis=0, keepdims=True)
        base = base + jnp.sum(onehot, axis=1, keepdims=True)
        vals.append(m)
        idxs.append(sel)
        ranks.append(rank)
    cnt_ref[0] = base.astype(jnp.int32)
    ex = [jnp.exp(vv - vals[0]) for vv in vals]
    den = ex[0] + ex[1] + ex[2] + ex[3]
    gate_ref[...] = jnp.concatenate(ex, axis=0) / den
    idx_ref[...] = jnp.concatenate(idxs, axis=0)
    rank_ref[...] = jnp.concatenate(ranks, axis=0).astype(jnp.int32)


def _router(x2, w_t, b_col):
    n, d = x2.shape
    e = w_t.shape[0]
    t = min(ROUTER_TILE, n)
    kt = pl.BlockSpec((TOP_K, t), lambda i: (0, i))
    return pl.pallas_call(
        _router_kernel,
        grid=(n // t,),
        in_specs=[
            pl.BlockSpec((t, d), lambda i: (i, 0)),
            pl.BlockSpec((e, d), lambda i: (0, 0)),
            pl.BlockSpec((e, 1), lambda i: (0, 0)),
        ],
        out_specs=[kt, kt, kt, pl.BlockSpec((1, e, 1), lambda i: (i, 0, 0))],
        out_shape=[
            jax.ShapeDtypeStruct((TOP_K, n), jnp.int32),
            jax.ShapeDtypeStruct((TOP_K, n), F32),
            jax.ShapeDtypeStruct((TOP_K, n), jnp.int32),
            jax.ShapeDtypeStruct((n // t, e, 1), jnp.int32),
        ],
        compiler_params=pltpu.CompilerParams(dimension_semantics=("arbitrary",)),
        name="moe_router",
    )(x2, w_t, b_col)


def _plan_kernel(cnt_ref, idx_ref, rank_ref, lpos_ref, tab_ref, blk_ref, fill_ref, acc_ref):
    i = pl.program_id(0)
    nt, e, _ = cnt_ref.shape
    t = idx_ref.shape[1]
    g_shift = ROW_GRANULE.bit_length() - 1
    b_shift = (EXPERT_ROWS // ROW_GRANULE).bit_length() - 1

    def units(cnt):
        return (cnt + (ROW_GRANULE - 1)) >> g_shift

    @pl.when(i == 0)
    def _():
        acc_ref[...] = jnp.zeros_like(acc_ref)

    total = units(cnt_ref[0])
    for j in range(1, nt):
        total = total + units(cnt_ref[j])
    pad_col = (((total + ((1 << b_shift) - 1)) >> b_shift) << b_shift).astype(F32)
    r = lax.broadcasted_iota(jnp.int32, (e, e), 0)
    c = lax.broadcasted_iota(jnp.int32, (e, e), 1)

    def to_row(col):
        return jnp.sum(jnp.where(r == c, col, 0.0), axis=0, keepdims=True)

    def prefix(col):
        return jnp.sum(jnp.where(c < r, to_row(col), 0.0), axis=1, keepdims=True)

    start_col = prefix(pad_col)
    end_col = start_col + pad_col
    used_col = start_col + total.astype(F32)

    mine = units(cnt_ref[i]).astype(F32)
    local_col = prefix(mine)
    global_col = start_col + acc_ref[...]
    acc_ref[...] = acc_ref[...] + mine
    tab_ref[0, 0:1, :] = to_row(local_col).astype(jnp.int32)
    tab_ref[0, 1:2, :] = to_row(mine).astype(jnp.int32)
    tab_ref[0, 2:3, :] = to_row(global_col).astype(jnp.int32)

    nb = blk_ref.shape[1]
    starts = (lax.broadcasted_iota(jnp.int32, (e, nb), 1) << b_shift).astype(F32)
    owner = jnp.sum(jnp.where(end_col <= starts, 1.0, 0.0), axis=0, keepdims=True)
    blk_ref[...] = jnp.minimum(owner, e - 1.0).astype(jnp.int32)

    lane = lax.broadcasted_iota(jnp.int32, (e, fill_ref.shape[1]), 1)
    sub = lax.broadcasted_iota(jnp.int32, (e, fill_ref.shape[1]), 0)
    fill = jnp.where(lane == sub, used_col, 0.0) + jnp.where(lane == sub + e, end_col, 0.0)
    fill_ref[...] = jnp.sum(fill, axis=0, keepdims=True).astype(jnp.int32) * ROW_GRANULE

    eid = lax.broadcasted_iota(jnp.int32, (e, t), 0)
    for kk in range(TOP_K):
        hit = eid == idx_ref[kk:kk + 1, :]
        rank = rank_ref[kk:kk + 1, :]
        l_k = jnp.sum(jnp.where(hit, local_col, 0.0), axis=0, keepdims=True).astype(jnp.int32)
        lpos_ref[kk:kk + 1, :] = l_k * ROW_GRANULE + rank


def _plan(counts, idx, rank, n_blocks):
    nt, e, _ = counts.shape
    n = idx.shape[1]
    t = n // nt
    kt = pl.BlockSpec((TOP_K, t), lambda i: (0, i))
    fill_lanes = V7X_LANES * (-(-2 * e // V7X_LANES))
    return pl.pallas_call(
        _plan_kernel,
        grid=(nt,),
        in_specs=[pl.BlockSpec((nt, e, 1), lambda i: (0, 0, 0)), kt, kt],
        out_specs=[
            kt,
            pl.BlockSpec((1, 3, e), lambda i: (i, 0, 0)),
            pl.BlockSpec((1, n_blocks), lambda i: (0, 0)),
            pl.BlockSpec((1, fill_lanes), lambda i: (0, 0)),
        ],
        out_shape=[
            jax.ShapeDtypeStruct((TOP_K, n), jnp.int32),
            jax.ShapeDtypeStruct((nt, 3, e), jnp.int32),
            jax.ShapeDtypeStruct((1, n_blocks), jnp.int32),
            jax.ShapeDtypeStruct((1, fill_lanes), jnp.int32),
        ],
        scratch_shapes=[pltpu.VMEM((e, 1), F32)],
        compiler_params=pltpu.CompilerParams(dimension_semantics=("arbitrary",)),
        name="moe_plan",
    )(counts, idx, rank)


def _slab_copies(fn, tab_ref, step, n_experts, local_ref, slot, sorted_ref, sem, to_sorted):
    def per_expert(ee, carry):
        base = step * (3 * n_experts)
        src_u = tab_ref[base + ee]
        n_u = tab_ref[base + n_experts + ee]
        dst_u = tab_ref[base + 2 * n_experts + ee]
        for b in range(SLAB_BITS):
            @pl.when(((n_u >> b) & 1) == 1)
            def _():
                off = n_u & ((1 << b) - 1)
                rows = ROW_GRANULE << b
                loc = local_ref.at[slot, pl.ds(pl.multiple_of((src_u + off) * ROW_GRANULE, ROW_GRANULE), rows)]
                glob = sorted_ref.at[pl.ds(pl.multiple_of((dst_u + off) * ROW_GRANULE, ROW_GRANULE), rows)]
                fn(pltpu.make_async_copy(loc, glob, sem.at[slot]) if to_sorted
                   else pltpu.make_async_copy(glob, loc, sem.at[slot]))
        return carry

    lax.fori_loop(0, n_experts, per_expert, 0)


def _dispatch_kernel(tab_ref, fill_ref, x_ref, lpos_ref, xs_out, xloc_ref, zero_ref, sem_run, sem_zero,
                     *, n_experts):
    t = x_ref.shape[0]
    n_rows = xs_out.shape[0]
    local_rows = xloc_ref.shape[1]
    i = pl.program_id(0)
    slot = i % 2

    @pl.when(i == 0)
    def _():
        zero_ref[...] = jnp.zeros_like(zero_ref)
        tail_blk = fill_ref[2 * n_experts - 1] // EXPERT_ROWS

        def fill_copies(fn):
            def per_expert(ee, carry):
                def one(uu, c2):
                    r0 = pl.multiple_of(uu * ROW_GRANULE, ROW_GRANULE)
                    fn(pltpu.make_async_copy(zero_ref.at[pl.ds(0, ROW_GRANULE)],
                                             xs_out.at[pl.ds(r0, ROW_GRANULE)], sem_zero))
                    return c2
                lax.fori_loop(fill_ref[ee] // ROW_GRANULE, fill_ref[n_experts + ee] // ROW_GRANULE, one, 0)
                return carry
            lax.fori_loop(0, n_experts, per_expert, 0)

            def per_tail(bi, carry):
                r0 = pl.multiple_of(bi * EXPERT_ROWS, EXPERT_ROWS)
                fn(pltpu.make_async_copy(zero_ref, xs_out.at[pl.ds(r0, EXPERT_ROWS)], sem_zero))
                return carry
            lax.fori_loop(tail_blk, n_rows // EXPERT_ROWS, per_tail, 0)

        fill_copies(lambda cp: cp.start())
        fill_copies(lambda cp: cp.wait())

    xb = x_ref[...].astype(BF16)
    lp = lpos_ref[...]
    for rc in range(local_rows // SORT_CHUNK):
        rows = lax.broadcasted_iota(jnp.int32, (SORT_CHUNK, t), 0) + rc * SORT_CHUNK
        sel = jnp.zeros((SORT_CHUNK, t), F32)
        for kk in range(TOP_K):
            sel = jnp.where(rows == lp[kk:kk + 1, :], 1.0, sel)
        xloc_ref[slot, SORT_CHUNK * rc:SORT_CHUNK * (rc + 1), :] = _dot(sel.astype(BF16), xb)

    move = functools.partial(_slab_copies, tab_ref=tab_ref, n_experts=n_experts, local_ref=xloc_ref,
                             sorted_ref=xs_out, sem=sem_run, to_sorted=True)
    move(lambda cp: cp.start(), step=i, slot=slot)

    @pl.when(i > 0)
    def _():
        move(lambda cp: cp.wait(), step=i - 1, slot=1 - slot)

    @pl.when(i == pl.num_programs(0) - 1)
    def _():
        move(lambda cp: cp.wait(), step=i, slot=slot)


def _dispatch(x2, lpos, tab, fill, n_rows, n_experts):
    n, d = x2.shape
    nt = tab.shape[0] // (3 * n_experts)
    t = n // nt
    local_rows = -(-(TOP_K * t + (ROW_GRANULE - 1) * n_experts) // SORT_CHUNK) * SORT_CHUNK
    grid_spec = pltpu.PrefetchScalarGridSpec(
        num_scalar_prefetch=2,
        grid=(nt,),
        in_specs=[
            pl.BlockSpec((t, d), lambda i, tb, fl: (i, 0)),
            pl.BlockSpec((TOP_K, t), lambda i, tb, fl: (0, i)),
        ],
        out_specs=pl.BlockSpec(memory_space=pl.ANY),
        scratch_shapes=[
            pltpu.VMEM((2, local_rows, d), F32),
            pltpu.VMEM((EXPERT_ROWS, d), F32),
            pltpu.SemaphoreType.DMA((2,)),
            pltpu.SemaphoreType.DMA,
        ],
    )
    return pl.pallas_call(
        functools.partial(_dispatch_kernel, n_experts=n_experts),
        grid_spec=grid_spec,
        out_shape=jax.ShapeDtypeStruct((n_rows, d), F32),
        compiler_params=pltpu.CompilerParams(
            dimension_semantics=("arbitrary",), vmem_limit_bytes=V7X_VMEM_LIMIT),
        name="moe_dispatch",
    )(tab, fill, x2, lpos)


def _expert_kernel(blk_e_ref, fill_ref, xs_ref, wgu_ref, bg_ref, bl_ref, wd_ref, bd_ref, y_ref,
                   wt_ref, wg_s, wl_s, wd_s, *, n_experts):
    i = pl.program_id(0)
    ff, d = wg_s.shape
    tc, rk = wt_ref.shape
    half = tc // 2
    changed = blk_e_ref[i] != blk_e_ref[jnp.maximum(i - 1, 0)]
    used = i * EXPERT_ROWS < fill_ref[2 * n_experts - 1]

    @pl.when((i == 0) | changed)
    def _():
        for kb in range(d // rk):
            for j in range(2 * ff // tc):
                wt_ref[...] = wgu_ref[0, 0, rk * kb:rk * (kb + 1), tc * j:tc * (j + 1)].T
                wg_s[half * j:half * (j + 1), rk * kb:rk * (kb + 1)] = (
                    wt_ref[pl.ds(0, half, stride=2), :].astype(BF16))
                wl_s[half * j:half * (j + 1), rk * kb:rk * (kb + 1)] = (
                    wt_ref[pl.ds(1, half, stride=2), :].astype(BF16))
        wd_s[...] = wd_ref[0, 0].astype(BF16)

    @pl.when(used)
    def _():
        xb = xs_ref[...].astype(BF16)
        hg = _dot(xb, wg_s[...], _NT) + bg_ref[0]
        hl = _dot(xb, wl_s[...], _NT) + bl_ref[0]
        glu = jnp.minimum(hg, SWIGLU_LIMIT)
        lin = jnp.clip(hl, -SWIGLU_LIMIT, SWIGLU_LIMIT)
        a = glu * jax.nn.sigmoid(SWIGLU_ALPHA * glu) * (lin + 1.0)
        y_ref[...] = _dot(a.astype(BF16), wd_s[...]) + bd_ref[0, 0]

    @pl.when(jnp.logical_not(used))
    def _():
        y_ref[...] = jnp.zeros_like(y_ref)


def _experts(xs, blk_e, fill, w_gu, b_glu, b_lin, w_dn, b_dn, layer):
    n_rows, d = xs.shape
    ff = w_dn.shape[2]
    e = w_dn.shape[1]
    bm = EXPERT_ROWS
    wspec = lambda r, c: pl.BlockSpec((1, r, c), lambda i, be, fl: (be[i], 0, 0))
    lspec = lambda r, c: pl.BlockSpec((1, 1, r, c), lambda i, be, fl: (layer, be[i], 0, 0))
    grid_spec = pltpu.PrefetchScalarGridSpec(
        num_scalar_prefetch=2,
        grid=(n_rows // bm,),
        in_specs=[
            pl.BlockSpec((bm, d), lambda i, be, fl: (i, 0)),
            lspec(d, 2 * ff), wspec(1, ff), wspec(1, ff),
            lspec(ff, d), lspec(1, d),
        ],
        out_specs=pl.BlockSpec((bm, d), lambda i, be, fl: (i, 0)),
        scratch_shapes=[
            pltpu.VMEM((WEIGHT_SPLIT_COLS, V7X_LANES), F32),
            pltpu.VMEM((ff, d), BF16),
            pltpu.VMEM((ff, d), BF16),
            pltpu.VMEM((ff, d), BF16),
        ],
    )
    return pl.pallas_call(
        functools.partial(_expert_kernel, n_experts=e),
        grid_spec=grid_spec,
        out_shape=jax.ShapeDtypeStruct((n_rows, d), F32),
        compiler_params=pltpu.CompilerParams(
            dimension_semantics=("arbitrary",), vmem_limit_bytes=V7X_VMEM_LIMIT),
        name="moe_experts",
    )(blk_e, fill, xs, w_gu, b_glu, b_lin, w_dn, b_dn)


def _combine_kernel(tab_ref, y_hbm, x_ref, lpos_ref, gate_ref, lg_ref, lbias_ref, o_ref,
                    yloc_ref, sem_run, *, dn_alpha, n_experts):
    t = x_ref.shape[0]
    local_rows = yloc_ref.shape[1]
    i = pl.program_id(0)
    slot = i % 2
    fetch = functools.partial(_slab_copies, tab_ref=tab_ref, n_experts=n_experts, local_ref=yloc_ref,
                              sorted_ref=y_hbm, sem=sem_run, to_sorted=False)

    @pl.when(i == 0)
    def _():
        yloc_ref[...] = jnp.zeros_like(yloc_ref)
        fetch(lambda cp: cp.start(), step=0, slot=0)

    @pl.when(i + 1 < pl.num_programs(0))
    def _():
        fetch(lambda cp: cp.start(), step=i + 1, slot=1 - slot)

    fetch(lambda cp: cp.wait(), step=i, slot=slot)

    lp = lpos_ref[...]
    gate = gate_ref[...]
    ffn = jnp.zeros(x_ref.shape, F32)
    for rc in range(local_rows // SORT_CHUNK):
        cols = lax.broadcasted_iota(jnp.int32, (t, SORT_CHUNK), 1) + rc * SORT_CHUNK
        w = jnp.zeros((t, SORT_CHUNK), F32)
        for kk in range(TOP_K):
            w = jnp.where(cols == lp[:, kk:kk + 1], gate[:, kk:kk + 1], w)
        y_c = yloc_ref[slot, SORT_CHUNK * rc:SORT_CHUNK * (rc + 1), :].astype(BF16)
        ffn = ffn + _dot(w.astype(BF16), y_c)
    o_ref[...] = _layer_norm(dn_alpha * x_ref[...] + ffn, lg_ref[...], lbias_ref[...])


def _combine(y, x2, lpos_t, gate_t, tab, ln_g, ln_b, dn_alpha, n_experts):
    n, d = x2.shape
    nt = tab.shape[0] // (3 * n_experts)
    t = n // nt
    local_rows = -(-(TOP_K * t + (ROW_GRANULE - 1) * n_experts) // SORT_CHUNK) * SORT_CHUNK
    vec = pl.BlockSpec((1, d), lambda i, tb: (0, 0))
    tk = pl.BlockSpec((t, TOP_K), lambda i, tb: (i, 0))
    grid_spec = pltpu.PrefetchScalarGridSpec(
        num_scalar_prefetch=1,
        grid=(nt,),
        in_specs=[
            pl.BlockSpec(memory_space=pl.ANY),
            pl.BlockSpec((t, d), lambda i, tb: (i, 0)),
            tk, tk, vec, vec,
        ],
        out_specs=pl.BlockSpec((t, d), lambda i, tb: (i, 0)),
        scratch_shapes=[
            pltpu.VMEM((2, local_rows, d), F32),
            pltpu.SemaphoreType.DMA((2,)),
        ],
    )
    return pl.pallas_call(
        functools.partial(_combine_kernel, dn_alpha=dn_alpha, n_experts=n_experts),
        grid_spec=grid_spec,
        out_shape=jax.ShapeDtypeStruct((n, d), F32),
        compiler_params=pltpu.CompilerParams(
            dimension_semantics=("arbitrary",), vmem_limit_bytes=V7X_VMEM_LIMIT),
        name="moe_combine",
    )(tab, y, x2, lpos_t, gate_t, ln_g, ln_b)


def _moe_layer(x2, w_r, b_r, w_gu, b_glu, b_lin, w_dn, b_dn, layer, ln_g, ln_b, dn_alpha):
    n, d = x2.shape
    e = w_r.shape[1]
    idx, gate, rank, counts = _router(x2, w_r.T, b_r.reshape(e, 1))
    n_blocks = -(-(n * TOP_K + (ROW_GRANULE - 1) * counts.shape[0] * e) // EXPERT_ROWS) + e
    lpos, tab, blk_e, fill = _plan(counts, idx, rank, n_blocks)
    fill = fill.reshape(-1)
    tab = tab.reshape(-1)
    xs = _dispatch(x2, lpos, tab, fill, n_blocks * EXPERT_ROWS, e)
    y = _experts(xs, blk_e.reshape(-1), fill, w_gu, b_glu, b_lin, w_dn, b_dn, layer)
    return _combine(y, x2, lpos.T, gate.T, tab, ln_g, ln_b, dn_alpha, e)


def kernel(x, hgrn_w_in, hgrn_norm_g, hgrn_w_out, lower_bounds, ret_w_in, ret_w_out, ln_mix_g, ln_mix_b,
           ln_ffn_g, ln_ffn_b, router_w, router_b, expert_w_gu, expert_b_gu, expert_w_down, expert_b_down):
    b, s, d = x.shape
    depth = lower_bounds.shape[0]
    dn_alpha = (2 * depth) ** 0.25
    lb_soft = jax.nn.softmax(lower_bounds.astype(F32), axis=0)
    lbs = jnp.cumsum(lb_soft, axis=0) - lb_soft[0]
    cos, sin_e, sin_o = _rotary_tables(s, d // RET_HEADS)
    row = lambda p, i: p[i].reshape(1, -1)
    for i in range(depth):
        j = i // 2
        if i % 2 == 0:
            x = _hgrn_layer(x, hgrn_w_in[j].astype(BF16), row(lbs, i), row(hgrn_norm_g, j),
                            hgrn_w_out[j].astype(BF16), row(ln_mix_g, i), row(ln_mix_b, i), dn_alpha)
        else:
            x = _ret_layer(x, ret_w_in[j].astype(BF16), cos, sin_e, sin_o, ret_w_out[j].astype(BF16),
                           row(ln_mix_g, i), row(ln_mix_b, i), dn_alpha)
        b_gu = expert_b_gu[i]
        x2 = _moe_layer(
            x.reshape(b * s, d), router_w[i], router_b[i], expert_w_gu,
            b_gu[:, None, 0::2], b_gu[:, None, 1::2],
            expert_w_down, expert_b_down[:, :, None, :], i,
            row(ln_ffn_g, i), row(ln_ffn_b, i), dn_alpha)
        x = x2.reshape(b, s, d)
    return x
```

```python
import functools
import math

import jax
import jax.numpy as jnp
from jax import lax
from jax.experimental import pallas as pl
from jax.experimental.pallas import tpu as pltpu

HGRN_HEAD = 128
RET_HEADS = 4
ROPE_BASE = 10000.0
N_EXPERTS = 32
TOP_K = 4
SWIGLU_LIMIT = 7.0
SWIGLU_ALPHA = 1.702
LN_EPS = 1e-5
RMS_EPS = 1e-6

V7X_VMEM_LIMIT = 56 * 1024 * 1024
V7X_LANES = 128
SEQ_TILE = 256
HGRN_CHUNK = 64
HGRN_SUB = 16
RET_CHUNK = 128
ROUTER_TILE = 512
EXPERT_ROWS = 512
ROW_GRANULE = 8
SLAB_BITS = (ROUTER_TILE // ROW_GRANULE).bit_length()
SORT_CHUNK = 256
PROJ_PIECE = 256
WEIGHT_SPLIT_COLS = 256

F32 = jnp.float32
BF16 = jnp.bfloat16

_NT = (((1,), (1,)), ((), ()))
_TN = (((0,), (0,)), ((), ()))


def _dot(a, b, dims=None):
    if dims is None:
        return jnp.dot(a, b, preferred_element_type=F32)
    return lax.dot_general(a, b, dims, preferred_element_type=F32)


def _layer_norm(v, g, b):
    mu = jnp.mean(v, axis=-1, keepdims=True)
    d = v - mu
    var = jnp.mean(d * d, axis=-1, keepdims=True)
    return d * lax.rsqrt(var + LN_EPS) * g + b


def _split3(v):
    hi = v.astype(BF16)
    r1 = v - hi.astype(F32)
    mid = r1.astype(BF16)
    lo = (r1 - mid.astype(F32)).astype(BF16)
    return hi, mid, lo


def _hgrn_kernel(x_ref, xn_ref, w_in_ref, lb_ref, ng_ref, w_out_ref, lg_ref, lbias_ref,
                 o_ref, ha_ref, hb_ref, xa_ref, xb_ref, oacc_ref, og_ref, st_ref, qa_ref, k_ref, b_ref,
                 *, dn_alpha):
    ts, d = xn_ref.shape[1], xn_ref.shape[2]
    n_heads = d // HGRN_HEAD
    c = HGRN_CHUNK
    n_sub = c // HGRN_SUB
    cur = {}

    @pl.when(pl.program_id(1) == 0)
    def _():
        st_ref[...] = jnp.zeros_like(st_ref)

    @pl.when((pl.program_id(0) == 0) & (pl.program_id(1) == 0))
    def _():
        ha_ref[...] = _dot(x_ref[0, 0:ts, :].astype(BF16), w_in_ref[...])

    n_pieces = 4 * d // PROJ_PIECE

    def project(piece):
        cols = slice(PROJ_PIECE * piece, PROJ_PIECE * (piece + 1))
        cur["h_next"][:, cols] = _dot(cur["x_next"][...], w_in_ref[:, cols])

    sub = HGRN_SUB
    row = lax.broadcasted_iota(jnp.int32, (c, c), 0)
    col = lax.broadcasted_iota(jnp.int32, (c, c), 1)
    sub_shift = sub.bit_length() - 1
    same_sub = (row >> sub_shift) == (col >> sub_shift)
    tril = jnp.where(same_sub & (row >= col), 1.0, 0.0).astype(BF16)

    def gates(rows):
        lb = lb_ref[...]
        one_m_lb = 1.0 - lb
        q = cur["h"][rows, 0:d]
        sf = cur["h"][rows, d:2 * d]
        qa_ref[...] = q * jax.nn.sigmoid(q)
        k_ref[...] = one_m_lb * jax.nn.sigmoid(-sf)
        logf = jnp.log(lb + one_m_lb * jax.nn.sigmoid(sf))
        hi = logf.astype(BF16)
        lo = (logf - hi.astype(F32)).astype(BF16)
        b_ref[...] = _dot(tril, lo) + _dot(tril, hi)

    def head_scores(rows, hd):
        lo_, hi_ = HGRN_HEAD * hd, HGRN_HEAD * (hd + 1)
        b = b_ref[:, lo_:hi_]
        vb = cur["h"][rows, 2 * d + lo_:2 * d + hi_].astype(BF16)
        tot = [b[sub * j + sub - 1:sub * j + sub, :] for j in range(n_sub)]
        to_end = jnp.concatenate([jnp.broadcast_to(t, (sub, HGRN_HEAD)) for t in tot], axis=0) - b
        q_dec = qa_ref[:, lo_:hi_] * jnp.exp(b)
        k_st = k_ref[:, lo_:hi_] * jnp.exp(to_end)
        g_start = [jnp.zeros_like(tot[0])]
        for j in range(n_sub):
            g_start.append(g_start[j] + tot[j])
        g_last = g_start[n_sub]
        qd = [q_dec[sub * j:sub * (j + 1), :] for j in range(n_sub)]
        ks = [k_st[sub * j:sub * (j + 1), :] for j in range(n_sub)]
        q_in = jnp.concatenate([qd[j] * jnp.exp(g_start[j]) for j in range(n_sub)], axis=0).astype(BF16)
        k_out = jnp.concatenate([ks[j] * jnp.exp(g_last - g_start[j + 1]) for j in range(n_sub)],
                                axis=0).astype(BF16)
        st = st_ref[hd]
        o_inter = _dot(q_in, st.astype(BF16), _NT)
        scs = []
        for i in range(n_sub):
            qm = jnp.concatenate([qd[j] * jnp.exp(g_start[j] - g_start[i + 1]) for j in range(i, n_sub)],
                                 axis=0).astype(BF16)
            scs.append(_dot(qm, ks[i].astype(BF16), _NT))
        new_st = st * jnp.exp(g_last) + _dot(vb, k_out, _TN)
        return o_inter, scs, vb, new_st

    def head_output(rows, hd, o_h, scs, vb, new_st):
        lo_, hi_ = HGRN_HEAD * hd, HGRN_HEAD * (hd + 1)
        for i, sc in enumerate(scs):
            rr = lax.broadcasted_iota(jnp.int32, sc.shape, 0)
            cc = lax.broadcasted_iota(jnp.int32, sc.shape, 1)
            sc = jnp.where(rr >= cc, sc, 0.0).astype(BF16)
            part = _dot(sc, vb[sub * i:sub * (i + 1), :])
            if i:
                part = jnp.concatenate([jnp.zeros((sub * i, HGRN_HEAD), F32), part], axis=0)
            o_h = o_h + part
        st_ref[hd] = new_st
        oacc_ref[rows, lo_:hi_] = o_h

    strip = 16

    n_chunks = ts // c
    stages = 4
    per_slot = -(-n_pieces // (n_chunks * stages))

    def tile_pass(r_base, h_ref, h_next, x_next_ref, x_next):
        cur.update(h=h_ref, h_next=h_next, x_next=x_next_ref)
        x_next_ref[...] = x_next.astype(BF16)
        pieces = iter(range(n_pieces))

        def project_some():
            for _ in range(per_slot):
                piece = next(pieces, None)
                if piece is not None:
                    project(piece)

        for ci in range(n_chunks):
            r0 = ci * c
            rows = pl.ds(r0, c)
            project_some()
            gates(rows)
            project_some()
            staged = [head_scores(rows, hd) for hd in range(n_heads // 2)]
            project_some()
            staged += [head_scores(rows, hd) for hd in range(n_heads // 2, n_heads)]
            project_some()
            for hd in range(n_heads):
                head_output(rows, hd, *staged[hd])
            for si in range(c // strip):
                rs = pl.ds(r0 + si * strip, strip)
                o = oacc_ref[rs, :]
                g = h_ref[rs, 3 * d:4 * d]
                o = o * lax.rsqrt(jnp.mean(o * o, axis=-1, keepdims=True) + RMS_EPS) * ng_ref[...]
                og_ref[rs, :] = (o * (g * jax.nn.sigmoid(g))).astype(BF16)
        for piece in pieces:
            project(piece)
        mix = _dot(og_ref[...], w_out_ref[...])
        o_ref[0, r_base:r_base + ts, :] = _layer_norm(
            dn_alpha * x_ref[0, r_base:r_base + ts, :] + mix, lg_ref[...], lbias_ref[...])

    tile_pass(0, ha_ref, hb_ref, xa_ref, x_ref[0, ts:2 * ts, :])
    tile_pass(ts, hb_ref, ha_ref, xb_ref, xn_ref[0])


def _next_pair_spec(n_batch, n_pairs, ts, d):
    def index(bi, si):
        nxt = jnp.minimum(bi * n_pairs + si + 1, n_batch * n_pairs - 1)
        return (nxt // n_pairs, 2 * (nxt % n_pairs), 0)
    return pl.BlockSpec((1, ts, d), index)


def _hgrn_layer(x, w_in, lb, norm_g, w_out, ln_g, ln_b, dn_alpha):
    b, s, d = x.shape
    ts = min(SEQ_TILE, s // 2)
    n_pairs = s // (2 * ts)
    vec = pl.BlockSpec((1, d), lambda bi, si: (0, 0))
    return pl.pallas_call(
        functools.partial(_hgrn_kernel, dn_alpha=dn_alpha),
        grid=(b, n_pairs),
        in_specs=[
            pl.BlockSpec((1, 2 * ts, d), lambda bi, si: (bi, si, 0)),
            _next_pair_spec(b, n_pairs, ts, d),
            pl.BlockSpec((d, 4 * d), lambda bi, si: (0, 0)),
            vec, vec,
            pl.BlockSpec((d, d), lambda bi, si: (0, 0)),
            vec, vec,
        ],
        out_specs=pl.BlockSpec((1, 2 * ts, d), lambda bi, si: (bi, si, 0)),
        out_shape=jax.ShapeDtypeStruct((b, s, d), F32),
        scratch_shapes=[
            pltpu.VMEM((ts, 4 * d), F32),
            pltpu.VMEM((ts, 4 * d), F32),
            pltpu.VMEM((ts, d), BF16),
            pltpu.VMEM((ts, d), BF16),
            pltpu.VMEM((ts, d), F32),
            pltpu.VMEM((ts, d), BF16),
            pltpu.VMEM((d // HGRN_HEAD, HGRN_HEAD, HGRN_HEAD), F32),
            pltpu.VMEM((HGRN_CHUNK, d), F32),
            pltpu.VMEM((HGRN_CHUNK, d), F32),
            pltpu.VMEM((HGRN_CHUNK, d), F32),
        ],
        compiler_params=pltpu.CompilerParams(
            dimension_semantics=("arbitrary", "arbitrary"),
            vmem_limit_bytes=V7X_VMEM_LIMIT),
        name="hgrn_mixer",
    )(x, x, w_in, lb, norm_g, w_out, ln_g, ln_b)


def _ret_kernel(x_ref, xn_ref, w_in_ref, cos_ref, sin_e_ref, sin_o_ref, w_out_ref, lg_ref, lbias_ref,
                o_ref, ha_ref, hb_ref, xa_ref, xb_ref, og_ref, st_ref, *, dn_alpha):
    ts, d = xn_ref.shape[1], xn_ref.shape[2]
    qk = d
    vd = 2 * d
    hk = qk // RET_HEADS
    hv = vd // RET_HEADS
    c = RET_CHUNK
    n_pieces = (2 * qk + 2 * vd) // PROJ_PIECE
    n_chunks = ts // c
    stages = 4
    per_slot = -(-n_pieces // (n_chunks * stages))

    @pl.when(pl.program_id(1) == 0)
    def _():
        st_ref[...] = jnp.zeros_like(st_ref)

    @pl.when((pl.program_id(0) == 0) & (pl.program_id(1) == 0))
    def _():
        ha_ref[...] = _dot(x_ref[0, 0:ts, :].astype(BF16), w_in_ref[...])

    ri = lax.broadcasted_iota(jnp.int32, (c, c), 0)
    cj = lax.broadcasted_iota(jnp.int32, (c, c), 1)
    rel = (ri - cj).astype(F32)
    idx_col = lax.broadcasted_iota(jnp.int32, (c, 1), 0).astype(F32)

    def chunk(r_base, ci, h_ref, project_some):
        rows = pl.ds(ci * c, c)
        pos = pl.ds(r_base + ci * c, c)
        project_some()
        cos = jnp.tile(cos_ref[pos, :], (1, RET_HEADS))
        sin_e = jnp.tile(sin_e_ref[pos, :], (1, RET_HEADS))
        sin_o = jnp.tile(sin_o_ref[pos, :], (1, RET_HEADS))

        def rope(t):
            nxt = pltpu.roll(t, qk - 1, 1)
            prv = pltpu.roll(t, 1, 1)
            return t * cos + nxt * sin_e + prv * sin_o

        q = rope(h_ref[rows, 0:qk]).astype(BF16)
        k = rope(h_ref[rows, qk:2 * qk]) * (hk ** -0.5)
        v = h_ref[rows, 2 * qk:2 * qk + vd].astype(BF16)
        lgam = [math.log(1.0 - 2.0 ** (-5.0 - hd)) for hd in range(RET_HEADS)]
        project_some()
        staged = []
        for hd in range(RET_HEADS):
            zeta = jnp.exp((c - 1.0 - idx_col) * lgam[hd])
            q_h = q[:, hk * hd:hk * (hd + 1)]
            k_h = k[:, hk * hd:hk * (hd + 1)]
            v_h = v[:, hv * hd:hv * (hd + 1)]
            st = st_ref[hd]
            sc = _dot(q_h, k_h.astype(BF16), _NT)
            o_inter = _dot(q_h, st.astype(BF16))
            new_st = st * math.exp(c * lgam[hd]) + _dot((k_h * zeta).astype(BF16), v_h, _TN)
            staged.append((sc, o_inter, new_st, v_h))
            if hd == RET_HEADS // 2 - 1:
                project_some()
        project_some()
        for hd, (sc, o_inter, new_st, v_h) in enumerate(staged):
            decay = jnp.where(rel >= 0, jnp.exp(rel * lgam[hd]), 0.0)
            xi = jnp.exp((idx_col + 1.0) * lgam[hd])
            o_h = _dot((sc * decay).astype(BF16), v_h) + o_inter * xi
            st_ref[hd] = new_st
            mu = jnp.mean(o_h, axis=-1, keepdims=True)
            dlt = o_h - mu
            var = jnp.mean(dlt * dlt, axis=-1, keepdims=True)
            g_h = h_ref[rows, 2 * qk + vd + hv * hd:2 * qk + vd + hv * (hd + 1)]
            og_ref[rows, hv * hd:hv * (hd + 1)] = (
                (g_h * jax.nn.sigmoid(g_h)) * (dlt * lax.rsqrt(var + LN_EPS))).astype(BF16)

    def tile_pass(r_base, h_ref, h_next, x_next_ref, x_next):
        x_next_ref[...] = x_next.astype(BF16)
        pieces = iter(range(n_pieces))

        def project(piece):
            cols = slice(PROJ_PIECE * piece, PROJ_PIECE * (piece + 1))
            h_next[:, cols] = _dot(x_next_ref[...], w_in_ref[:, cols])

        def project_some():
            for _ in range(per_slot):
                piece = next(pieces, None)
                if piece is not None:
                    project(piece)

        for ci in range(n_chunks):
            chunk(r_base, ci, h_ref, project_some)
        for piece in pieces:
            project(piece)
        mix = _dot(og_ref[...], w_out_ref[...])
        o_ref[0, r_base:r_base + ts, :] = _layer_norm(
            dn_alpha * x_ref[0, r_base:r_base + ts, :] + mix, lg_ref[...], lbias_ref[...])

    tile_pass(0, ha_ref, hb_ref, xa_ref, x_ref[0, ts:2 * ts, :])
    tile_pass(ts, hb_ref, ha_ref, xb_ref, xn_ref[0])


def _ret_layer(x, w_in, cos, sin_e, sin_o, w_out, ln_g, ln_b, dn_alpha):
    b, s, d = x.shape
    ts = min(SEQ_TILE, s // 2)
    n_pairs = s // (2 * ts)
    n_in = w_in.shape[1]
    hk = d // RET_HEADS
    vec = pl.BlockSpec((1, d), lambda bi, si: (0, 0))
    tab = pl.BlockSpec((2 * ts, hk), lambda bi, si: (si, 0))
    once = pl.Buffered(1)
    return pl.pallas_call(
        functools.partial(_ret_kernel, dn_alpha=dn_alpha),
        grid=(b, n_pairs),
        in_specs=[
            pl.BlockSpec((1, 2 * ts, d), lambda bi, si: (bi, si, 0)),
            _next_pair_spec(b, n_pairs, ts, d),
            pl.BlockSpec((d, n_in), lambda bi, si: (0, 0), pipeline_mode=once),
            tab, tab, tab,
            pl.BlockSpec((2 * d, d), lambda bi, si: (0, 0), pipeline_mode=once),
            vec, vec,
        ],
        out_specs=pl.BlockSpec((1, 2 * ts, d), lambda bi, si: (bi, si, 0)),
        out_shape=jax.ShapeDtypeStruct((b, s, d), F32),
        scratch_shapes=[
            pltpu.VMEM((ts, n_in), F32),
            pltpu.VMEM((ts, n_in), F32),
            pltpu.VMEM((ts, d), BF16),
            pltpu.VMEM((ts, d), BF16),
            pltpu.VMEM((ts, 2 * d), BF16),
            pltpu.VMEM((RET_HEADS, hk, 2 * d // RET_HEADS), F32),
        ],
        compiler_params=pltpu.CompilerParams(
            dimension_semantics=("arbitrary", "arbitrary"),
            vmem_limit_bytes=V7X_VMEM_LIMIT),
        name="ret_mixer",
    )(x, x, w_in, cos, sin_e, sin_o, w_out, ln_g, ln_b)


def _rotary_tables(seq_len, hk):
    pos = jnp.arange(seq_len, dtype=F32)
    inv = 1.0 / (ROPE_BASE ** jnp.linspace(0.0, 1.0, hk // 2, dtype=F32))
    theta = pos[:, None] * jnp.repeat(inv, 2)[None, :]
    cos, sin = jnp.cos(theta), jnp.sin(theta)
    even = (jnp.arange(hk) % 2 == 0)[None, :]
    return cos, jnp.where(even, -sin, 0.0), jnp.where(even, 0.0, sin)


def _router_kernel(x_ref, w_ref, b_ref, idx_ref, gate_ref, rank_ref, cnt_ref):
    t = x_ref.shape[0]
    e = w_ref.shape[0]

    xh, xm, xl = _split3(x_ref[...])
    wh, wm, wl = _split3(w_ref[...])
    logits = (_dot(wl, xh, _NT) + _dot(wh, xl, _NT) + _dot(wm, xm, _NT)
              + _dot(wm, xh, _NT) + _dot(wh, xm, _NT) + _dot(wh, xh, _NT)) + b_ref[...]

    eid = lax.broadcasted_iota(jnp.int32, (e, t), 0)
    ti = lax.broadcasted_iota(jnp.int32, (t, t), 0)
    tj = lax.broadcasted_iota(jnp.int32, (t, t), 1)
    before = jnp.where(ti < tj, 1.0, 0.0).astype(BF16)
    work = logits
    vals, idxs, ranks = [], [], []
    base = jnp.zeros((e, 1), F32)
    for _ in range(TOP_K):
        m = jnp.max(work, axis=0, keepdims=True)
        sel = jnp.min(jnp.where(work == m, eid, e), axis=0, keepdims=True)
        hit = eid == sel
        work = jnp.where(hit, -jnp.inf, work)
        onehot = jnp.where(hit, 1.0, 0.0)
        excl = _dot(onehot.astype(BF16), before)
        rank = jnp.sum(onehot * (excl + base), axis=0, keepdims=True)
        base = base + jnp.sum(onehot, axis=1, keepdims=True)
        vals.append(m)
        idxs.append(sel)
        ranks.append(rank)
    cnt_ref[0] = base.astype(jnp.int32)
    ex = [jnp.exp(vv - vals[0]) for vv in vals]
    den = ex[0] + ex[1] + ex[2] + ex[3]
    gate_ref[...] = jnp.concatenate(ex, axis=0) / den
    idx_ref[...] = jnp.concatenate(idxs, axis=0)
    rank_ref[...] = jnp.concatenate(ranks, axis=0).astype(jnp.int32)


def _router(x2, w_t, b_col):
    n, d = x2.shape
    e = w_t.shape[0]
    t = min(ROUTER_TILE, n)
    kt = pl.BlockSpec((TOP_K, t), lambda i: (0, i))
    return pl.pallas_call(
        _router_kernel,
        grid=(n // t,),
        in_specs=[
            pl.BlockSpec((t, d), lambda i: (i, 0)),
            pl.BlockSpec((e, d), lambda i: (0, 0)),
            pl.BlockSpec((e, 1), lambda i: (0, 0)),
        ],
        out_specs=[kt, kt, kt, pl.BlockSpec((1, e, 1), lambda i: (i, 0, 0))],
        out_shape=[
            jax.ShapeDtypeStruct((TOP_K, n), jnp.int32),
            jax.ShapeDtypeStruct((TOP_K, n), F32),
            jax.ShapeDtypeStruct((TOP_K, n), jnp.int32),
            jax.ShapeDtypeStruct((n // t, e, 1), jnp.int32),
        ],
        compiler_params=pltpu.CompilerParams(dimension_semantics=("arbitrary",)),
        name="moe_router",
    )(x2, w_t, b_col)


def _plan_kernel(cnt_ref, idx_ref, rank_ref, lpos_ref, tab_ref, blk_ref, fill_ref, acc_ref):
    i = pl.program_id(0)
    nt, e, _ = cnt_ref.shape
    t = idx_ref.shape[1]
    g_shift = ROW_GRANULE.bit_length() - 1
    b_shift = (EXPERT_ROWS // ROW_GRANULE).bit_length() - 1

    def units(cnt):
        return (cnt + (ROW_GRANULE - 1)) >> g_shift

    @pl.when(i == 0)
    def _():
        acc_ref[...] = jnp.zeros_like(acc_ref)

    total = units(cnt_ref[0])
    for j in range(1, nt):
        total = total + units(cnt_ref[j])
    pad_col = (((total + ((1 << b_shift) - 1)) >> b_shift) << b_shift).astype(F32)
    r = lax.broadcasted_iota(jnp.int32, (e, e), 0)
    c = lax.broadcasted_iota(jnp.int32, (e, e), 1)

    def to_row(col):
        return jnp.sum(jnp.where(r == c, col, 0.0), axis=0, keepdims=True)

    def prefix(col):
        return jnp.sum(jnp.where(c < r, to_row(col), 0.0), axis=1, keepdims=True)

    start_col = prefix(pad_col)
    end_col = start_col + pad_col
    used_col = start_col + total.astype(F32)

    mine = units(cnt_ref[i]).astype(F32)
    local_col = prefix(mine)
    global_col = start_col + acc_ref[...]
    acc_ref[...] = acc_ref[...] + mine
    tab_ref[0, 0:1, :] = to_row(local_col).astype(jnp.int32)
    tab_ref[0, 1:2, :] = to_row(mine).astype(jnp.int32)
    tab_ref[0, 2:3, :] = to_row(global_col).astype(jnp.int32)

    nb = blk_ref.shape[1]
    starts = (lax.broadcasted_iota(jnp.int32, (e, nb), 1) << b_shift).astype(F32)
    owner = jnp.sum(jnp.where(end_col <= starts, 1.0, 0.0), axis=0, keepdims=True)
    blk_ref[...] = jnp.minimum(owner, e - 1.0).astype(jnp.int32)

    lane = lax.broadcasted_iota(jnp.int32, (e, fill_ref.shape[1]), 1)
    sub = lax.broadcasted_iota(jnp.int32, (e, fill_ref.shape[1]), 0)
    fill = jnp.where(lane == sub, used_col, 0.0) + jnp.where(lane == sub + e, end_col, 0.0)
    fill_ref[...] = jnp.sum(fill, axis=0, keepdims=True).astype(jnp.int32) * ROW_GRANULE

    eid = lax.broadcasted_iota(jnp.int32, (e, t), 0)
    for kk in range(TOP_K):
        hit = eid == idx_ref[kk:kk + 1, :]
        rank = rank_ref[kk:kk + 1, :]
        l_k = jnp.sum(jnp.where(hit, local_col, 0.0), axis=0, keepdims=True).astype(jnp.int32)
        lpos_ref[kk:kk + 1, :] = l_k * ROW_GRANULE + rank


def _plan(counts, idx, rank, n_blocks):
    nt, e, _ = counts.shape
    n = idx.shape[1]
    t = n // nt
    kt = pl.BlockSpec((TOP_K, t), lambda i: (0, i))
    fill_lanes = V7X_LANES * (-(-2 * e // V7X_LANES))
    return pl.pallas_call(
        _plan_kernel,
        grid=(nt,),
        in_specs=[pl.BlockSpec((nt, e, 1), lambda i: (0, 0, 0)), kt, kt],
        out_specs=[
            kt,
            pl.BlockSpec((1, 3, e), lambda i: (i, 0, 0)),
            pl.BlockSpec((1, n_blocks), lambda i: (0, 0)),
            pl.BlockSpec((1, fill_lanes), lambda i: (0, 0)),
        ],
        out_shape=[
            jax.ShapeDtypeStruct((TOP_K, n), jnp.int32),
            jax.ShapeDtypeStruct((nt, 3, e), jnp.int32),
            jax.ShapeDtypeStruct((1, n_blocks), jnp.int32),
            jax.ShapeDtypeStruct((1, fill_lanes), jnp.int32),
        ],
        scratch_shapes=[pltpu.VMEM((e, 1), F32)],
        compiler_params=pltpu.CompilerParams(dimension_semantics=("arbitrary",)),
        name="moe_plan",
    )(counts, idx, rank)


def _slab_copies(fn, tab_ref, step, n_experts, local_ref, slot, sorted_ref, sem, to_sorted):
    def per_expert(ee, carry):
        base = step * (3 * n_experts)
        src_u = tab_ref[base + ee]
        n_u = tab_ref[base + n_experts + ee]
        dst_u = tab_ref[base + 2 * n_experts + ee]
        for b in range(SLAB_BITS):
            @pl.when(((n_u >> b) & 1) == 1)
            def _():
                off = n_u & ((1 << b) - 1)
                rows = ROW_GRANULE << b
                loc = local_ref.at[slot, pl.ds(pl.multiple_of((src_u + off) * ROW_GRANULE, ROW_GRANULE), rows)]
                glob = sorted_ref.at[pl.ds(pl.multiple_of((dst_u + off) * ROW_GRANULE, ROW_GRANULE), rows)]
                fn(pltpu.make_async_copy(loc, glob, sem.at[slot]) if to_sorted
                   else pltpu.make_async_copy(glob, loc, sem.at[slot]))
        return carry

    lax.fori_loop(0, n_experts, per_expert, 0)


def _dispatch_kernel(tab_ref, fill_ref, x_ref, lpos_ref, xs_out, xloc_ref, zero_ref, sem_run, sem_zero,
                     *, n_experts):
    t = x_ref.shape[0]
    n_rows = xs_out.shape[0]
    local_rows = xloc_ref.shape[1]
    i = pl.program_id(0)
    slot = i % 2

    @pl.when(i == 0)
    def _():
        zero_ref[...] = jnp.zeros_like(zero_ref)
        tail_blk = fill_ref[2 * n_experts - 1] // EXPERT_ROWS

        def fill_copies(fn):
            def per_expert(ee, carry):
                def one(uu, c2):
                    r0 = pl.multiple_of(uu * ROW_GRANULE, ROW_GRANULE)
                    fn(pltpu.make_async_copy(zero_ref.at[pl.ds(0, ROW_GRANULE)],
                                             xs_out.at[pl.ds(r0, ROW_GRANULE)], sem_zero))
                    return c2
                lax.fori_loop(fill_ref[ee] // ROW_GRANULE, fill_ref[n_experts + ee] // ROW_GRANULE, one, 0)
                return carry
            lax.fori_loop(0, n_experts, per_expert, 0)

            def per_tail(bi, carry):
                r0 = pl.multiple_of(bi * EXPERT_ROWS, EXPERT_ROWS)
                fn(pltpu.make_async_copy(zero_ref, xs_out.at[pl.ds(r0, EXPERT_ROWS)], sem_zero))
                return carry
            lax.fori_loop(tail_blk, n_rows // EXPERT_ROWS, per_tail, 0)

        fill_copies(lambda cp: cp.start())
        fill_copies(lambda cp: cp.wait())

    xb = x_ref[...].astype(BF16)
    lp = lpos_ref[...]
    for rc in range(local_rows // SORT_CHUNK):
        rows = lax.broadcasted_iota(jnp.int32, (SORT_CHUNK, t), 0) + rc * SORT_CHUNK
        sel = jnp.zeros((SORT_CHUNK, t), F32)
        for kk in range(TOP_K):
            sel = jnp.where(rows == lp[kk:kk + 1, :], 1.0, sel)
        xloc_ref[slot, SORT_CHUNK * rc:SORT_CHUNK * (rc + 1), :] = _dot(sel.astype(BF16), xb)

    move = functools.partial(_slab_copies, tab_ref=tab_ref, n_experts=n_experts, local_ref=xloc_ref,
                             sorted_ref=xs_out, sem=sem_run, to_sorted=True)
    move(lambda cp: cp.start(), step=i, slot=slot)

    @pl.when(i > 0)
    def _():
        move(lambda cp: cp.wait(), step=i - 1, slot=1 - slot)

    @pl.when(i == pl.num_programs(0) - 1)
    def _():
        move(lambda cp: cp.wait(), step=i, slot=slot)


def _dispatch(x2, lpos, tab, fill, n_rows, n_experts):
    n, d = x2.shape
    nt = tab.shape[0] // (3 * n_experts)
    t = n // nt
    local_rows = -(-(TOP_K * t + (ROW_GRANULE - 1) * n_experts) // SORT_CHUNK) * SORT_CHUNK
    grid_spec = pltpu.PrefetchScalarGridSpec(
        num_scalar_prefetch=2,
        grid=(nt,),
        in_specs=[
            pl.BlockSpec((t, d), lambda i, tb, fl: (i, 0)),
            pl.BlockSpec((TOP_K, t), lambda i, tb, fl: (0, i)),
        ],
        out_specs=pl.BlockSpec(memory_space=pl.ANY),
        scratch_shapes=[
            pltpu.VMEM((2, local_rows, d), F32),
            pltpu.VMEM((EXPERT_ROWS, d), F32),
            pltpu.SemaphoreType.DMA((2,)),
            pltpu.SemaphoreType.DMA,
        ],
    )
    return pl.pallas_call(
        functools.partial(_dispatch_kernel, n_experts=n_experts),
        grid_spec=grid_spec,
        out_shape=jax.ShapeDtypeStruct((n_rows, d), F32),
        compiler_params=pltpu.CompilerParams(
            dimension_semantics=("arbitrary",), vmem_limit_bytes=V7X_VMEM_LIMIT),
        name="moe_dispatch",
    )(tab, fill, x2, lpos)


def _expert_kernel(blk_e_ref, fill_ref, xs_ref, wgu_ref, bg_ref, bl_ref, wd_ref, bd_ref, y_ref,
                   wt_ref, wg_s, wl_s, wd_s, *, n_experts):
    i = pl.program_id(0)
    ff, d = wg_s.shape
    tc, rk = wt_ref.shape
    half = tc // 2
    changed = blk_e_ref[i] != blk_e_ref[jnp.maximum(i - 1, 0)]
    used = i * EXPERT_ROWS < fill_ref[2 * n_experts - 1]

    @pl.when((i == 0) | changed)
    def _():
        for kb in range(d // rk):
            for j in range(2 * ff // tc):
                wt_ref[...] = wgu_ref[0, 0, rk * kb:rk * (kb + 1), tc * j:tc * (j + 1)].T
                wg_s[half * j:half * (j + 1), rk * kb:rk * (kb + 1)] = (
                    wt_ref[pl.ds(0, half, stride=2), :].astype(BF16))
                wl_s[half * j:half * (j + 1), rk * kb:rk * (kb + 1)] = (
                    wt_ref[pl.ds(1, half, stride=2), :].astype(BF16))
        wd_s[...] = wd_ref[0, 0].astype(BF16)

    @pl.when(used)
    def _():
        xb = xs_ref[...].astype(BF16)
        hg = _dot(xb, wg_s[...], _NT) + bg_ref[0]
        hl = _dot(xb, wl_s[...], _NT) + bl_ref[0]
        glu = jnp.minimum(hg, SWIGLU_LIMIT)
        lin = jnp.clip(hl, -SWIGLU_LIMIT, SWIGLU_LIMIT)
        a = glu * jax.nn.sigmoid(SWIGLU_ALPHA * glu) * (lin + 1.0)
        y_ref[...] = _dot(a.astype(BF16), wd_s[...]) + bd_ref[0, 0]

    @pl.when(jnp.logical_not(used))
    def _():
        y_ref[...] = jnp.zeros_like(y_ref)


def _experts(xs, blk_e, fill, w_gu, b_glu, b_lin, w_dn, b_dn, layer):
    n_rows, d = xs.shape
    ff = w_dn.shape[2]
    e = w_dn.shape[1]
    bm = EXPERT_ROWS
    wspec = lambda r, c: pl.BlockSpec((1, r, c), lambda i, be, fl: (be[i], 0, 0))
    lspec = lambda r, c: pl.BlockSpec((1, 1, r, c), lambda i, be, fl: (layer, be[i], 0, 0))
    grid_spec = pltpu.PrefetchScalarGridSpec(
        num_scalar_prefetch=2,
        grid=(n_rows // bm,),
        in_specs=[
            pl.BlockSpec((bm, d), lambda i, be, fl: (i, 0)),
            lspec(d, 2 * ff), wspec(1, ff), wspec(1, ff),
            lspec(ff, d), lspec(1, d),
        ],
        out_specs=pl.BlockSpec((bm, d), lambda i, be, fl: (i, 0)),
        scratch_shapes=[
            pltpu.VMEM((WEIGHT_SPLIT_COLS, V7X_LANES), F32),
            pltpu.VMEM((ff, d), BF16),
            pltpu.VMEM((ff, d), BF16),
            pltpu.VMEM((ff, d), BF16),
        ],
    )
    return pl.pallas_call(
        functools.partial(_expert_kernel, n_experts=e),
        grid_spec=grid_spec,
        out_shape=jax.ShapeDtypeStruct((n_rows, d), F32),
        compiler_params=pltpu.CompilerParams(
            dimension_semantics=("arbitrary",), vmem_limit_bytes=V7X_VMEM_LIMIT),
        name="moe_experts",
    )(blk_e, fill, xs, w_gu, b_glu, b_lin, w_dn, b_dn)


def _combine_kernel(tab_ref, y_hbm, x_ref, lpos_ref, gate_ref, lg_ref, lbias_ref, o_ref,
                    yloc_ref, sem_run, *, dn_alpha, n_experts):
    t = x_ref.shape[0]
    local_rows = yloc_ref.shape[1]
    i = pl.program_id(0)
    slot = i % 2
    fetch = functools.partial(_slab_copies, tab_ref=tab_ref, n_experts=n_experts, local_ref=yloc_ref,
                              sorted_ref=y_hbm, sem=sem_run, to_sorted=False)

    @pl.when(i == 0)
    def _():
        yloc_ref[...] = jnp.zeros_like(yloc_ref)
        fetch(lambda cp: cp.start(), step=0, slot=0)

    @pl.when(i + 1 < pl.num_programs(0))
    def _():
        fetch(lambda cp: cp.start(), step=i + 1, slot=1 - slot)

    fetch(lambda cp: cp.wait(), step=i, slot=slot)

    lp = lpos_ref[...]
    gate = gate_ref[...]
    ffn = jnp.zeros(x_ref.shape, F32)
    for rc in range(local_rows // SORT_CHUNK):
        cols = lax.broadcasted_iota(jnp.int32, (t, SORT_CHUNK), 1) + rc * SORT_CHUNK
        w = jnp.zeros((t, SORT_CHUNK), F32)
        for kk in range(TOP_K):
            w = jnp.where(cols == lp[:, kk:kk + 1], gate[:, kk:kk + 1], w)
        y_c = yloc_ref[slot, SORT_CHUNK * rc:SORT_CHUNK * (rc + 1), :].astype(BF16)
        ffn = ffn + _dot(w.astype(BF16), y_c)
    o_ref[...] = _layer_norm(dn_alpha * x_ref[...] + ffn, lg_ref[...], lbias_ref[...])


def _combine(y, x2, lpos_t, gate_t, tab, ln_g, ln_b, dn_alpha, n_experts):
    n, d = x2.shape
    nt = tab.shape[0] // (3 * n_experts)
    t = n // nt
    local_rows = -(-(TOP_K * t + (ROW_GRANULE - 1) * n_experts) // SORT_CHUNK) * SORT_CHUNK
    vec = pl.BlockSpec((1, d), lambda i, tb: (0, 0))
    tk = pl.BlockSpec((t, TOP_K), lambda i, tb: (i, 0))
    grid_spec = pltpu.PrefetchScalarGridSpec(
        num_scalar_prefetch=1,
        grid=(nt,),
        in_specs=[
            pl.BlockSpec(memory_space=pl.ANY),
            pl.BlockSpec((t, d), lambda i, tb: (i, 0)),
            tk, tk, vec, vec,
        ],
        out_specs=pl.BlockSpec((t, d), lambda i, tb: (i, 0)),
        scratch_shapes=[
            pltpu.VMEM((2, local_rows, d), F32),
            pltpu.SemaphoreType.DMA((2,)),
        ],
    )
    return pl.pallas_call(
        functools.partial(_combine_kernel, dn_alpha=dn_alpha, n_experts=n_experts),
        grid_spec=grid_spec,
        out_shape=jax.ShapeDtypeStruct((n, d), F32),
        compiler_params=pltpu.CompilerParams(
            dimension_semantics=("arbitrary",), vmem_limit_bytes=V7X_VMEM_LIMIT),
        name="moe_combine",
    )(tab, y, x2, lpos_t, gate_t, ln_g, ln_b)


def _moe_layer(x2, w_r, b_r, w_gu, b_glu, b_lin, w_dn, b_dn, layer, ln_g, ln_b, dn_alpha):
    n, d = x2.shape
    e = w_r.shape[1]
    idx, gate, rank, counts = _router(x2, w_r.T, b_r.reshape(e, 1))
    n_blocks = -(-(n * TOP_K + (ROW_GRANULE - 1) * counts.shape[0] * e) // EXPERT_ROWS) + e
    lpos, tab, blk_e, fill = _plan(counts, idx, rank, n_blocks)
    fill = fill.reshape(-1)
    tab = tab.reshape(-1)
    xs = _dispatch(x2, lpos, tab, fill, n_blocks * EXPERT_ROWS, e)
    y = _experts(xs, blk_e.reshape(-1), fill, w_gu, b_glu, b_lin, w_dn, b_dn, layer)
    return _combine(y, x2, lpos.T, gate.T, tab, ln_g, ln_b, dn_alpha, e)


def kernel(x, hgrn_w_in, hgrn_norm_g, hgrn_w_out, lower_bounds, ret_w_in, ret_w_out, ln_mix_g, ln_mix_b,
           ln_ffn_g, ln_ffn_b, router_w, router_b, expert_w_gu, expert_b_gu, expert_w_down, expert_b_down):
    b, s, d = x.shape
    depth = lower_bounds.shape[0]
    dn_alpha = (2 * depth) ** 0.25
    lb_soft = jax.nn.softmax(lower_bounds.astype(F32), axis=0)
    lbs = jnp.cumsum(lb_soft, axis=0) - lb_soft[0]
    cos, sin_e, sin_o = _rotary_tables(s, d // RET_HEADS)
    row = lambda p, i: p[i].reshape(1, -1)
    for i in range(depth):
        j = i // 2
        if i % 2 == 0:
            x = _hgrn_layer(x, hgrn_w_in[j].astype(BF16), row(lbs, i), row(hgrn_norm_g, j),
                            hgrn_w_out[j].astype(BF16), row(ln_mix_g, i), row(ln_mix_b, i), dn_alpha)
        else:
            x = _ret_layer(x, ret_w_in[j].astype(BF16), cos, sin_e, sin_o, ret_w_out[j].astype(BF16),
                           row(ln_mix_g, i), row(ln_mix_b, i), dn_alpha)
        b_gu = expert_b_gu[i]
        x2 = _moe_layer(
            x.reshape(b * s, d), router_w[i], router_b[i], expert_w_gu,
            b_gu[:, None, 0::2], b_gu[:, None, 1::2],
            expert_w_down, expert_b_down[:, :, None, :], i,
            row(ln_ffn_g, i), row(ln_ffn_b, i), dn_alpha)
        x = x2.reshape(b, s, d)
    return x
```

```python
import functools
import math

import jax
import jax.numpy as jnp
from jax import lax
from jax.experimental import pallas as pl
from jax.experimental.pallas import tpu as pltpu

HGRN_HEAD = 128
RET_HEADS = 4
ROPE_BASE = 10000.0
TOP_K = 4
SWIGLU_LIMIT = 7.0
SWIGLU_ALPHA = 1.702
LN_EPS = 1e-5
RMS_EPS = 1e-6

V7X_VMEM_LIMIT = 56 * 1024 * 1024
V7X_LANES = 128
SEQ_TILE = 256
HGRN_CHUNK = 64
HGRN_SUB = 16
RET_CHUNK = 128
ROUTER_TILE = 512
EXPERT_ROWS = 512
ROW_GRANULE = 8
SLAB_BITS = (ROUTER_TILE // ROW_GRANULE).bit_length()
SORT_CHUNK = 256
PROJ_PIECE = 256
WEIGHT_SPLIT_COLS = 256

F32 = jnp.float32
BF16 = jnp.bfloat16

_NT = (((1,), (1,)), ((), ()))
_TN = (((0,), (0,)), ((), ()))


def _dot(a, b, dims=None):
    if dims is None:
        return jnp.dot(a, b, preferred_element_type=F32)
    return lax.dot_general(a, b, dims, preferred_element_type=F32)


def _layer_norm(v, g, b):
    mu = jnp.mean(v, axis=-1, keepdims=True)
    d = v - mu
    var = jnp.mean(d * d, axis=-1, keepdims=True)
    return d * lax.rsqrt(var + LN_EPS) * g + b


def _split3(v):
    hi = v.astype(BF16)
    r1 = v - hi.astype(F32)
    mid = r1.astype(BF16)
    lo = (r1 - mid.astype(F32)).astype(BF16)
    return hi, mid, lo


def _hgrn_kernel(x_ref, xn_ref, w_in_ref, lb_ref, ng_ref, w_out_ref, lg_ref, lbias_ref,
                 o_ref, ha_ref, hb_ref, xa_ref, xb_ref, oacc_ref, og_ref, st_ref, qa_ref, k_ref, b_ref,
                 *, dn_alpha):
    ts, d = xn_ref.shape[1], xn_ref.shape[2]
    n_heads = d // HGRN_HEAD
    c = HGRN_CHUNK
    n_sub = c // HGRN_SUB
    cur = {}

    @pl.when(pl.program_id(1) == 0)
    def _():
        st_ref[...] = jnp.zeros_like(st_ref)

    @pl.when((pl.program_id(0) == 0) & (pl.program_id(1) == 0))
    def _():
        ha_ref[...] = _dot(x_ref[0, 0:ts, :].astype(BF16), w_in_ref[...])

    n_pieces = 4 * d // PROJ_PIECE

    def project(piece):
        cols = slice(PROJ_PIECE * piece, PROJ_PIECE * (piece + 1))
        cur["h_next"][:, cols] = _dot(cur["x_next"][...], w_in_ref[:, cols])

    sub = HGRN_SUB
    row = lax.broadcasted_iota(jnp.int32, (c, c), 0)
    col = lax.broadcasted_iota(jnp.int32, (c, c), 1)
    sub_shift = sub.bit_length() - 1
    same_sub = (row >> sub_shift) == (col >> sub_shift)
    tril = jnp.where(same_sub & (row >= col), 1.0, 0.0).astype(BF16)

    def gates(rows):
        lb = lb_ref[...]
        one_m_lb = 1.0 - lb
        q = cur["h"][rows, 0:d]
        sf = cur["h"][rows, d:2 * d]
        qa_ref[...] = q * jax.nn.sigmoid(q)
        k_ref[...] = one_m_lb * jax.nn.sigmoid(-sf)
        logf = jnp.log(lb + one_m_lb * jax.nn.sigmoid(sf))
        hi = logf.astype(BF16)
        lo = (logf - hi.astype(F32)).astype(BF16)
        b_ref[...] = _dot(tril, lo) + _dot(tril, hi)

    def head_scores(rows, hd):
        lo_, hi_ = HGRN_HEAD * hd, HGRN_HEAD * (hd + 1)
        b = b_ref[:, lo_:hi_]
        vb = cur["h"][rows, 2 * d + lo_:2 * d + hi_].astype(BF16)
        tot = [b[sub * j + sub - 1:sub * j + sub, :] for j in range(n_sub)]
        to_end = jnp.concatenate([jnp.broadcast_to(t, (sub, HGRN_HEAD)) for t in tot], axis=0) - b
        q_dec = qa_ref[:, lo_:hi_] * jnp.exp(b)
        k_st = k_ref[:, lo_:hi_] * jnp.exp(to_end)
        g_start = [jnp.zeros_like(tot[0])]
        for j in range(n_sub):
            g_start.append(g_start[j] + tot[j])
        g_last = g_start[n_sub]
        qd = [q_dec[sub * j:sub * (j + 1), :] for j in range(n_sub)]
        ks = [k_st[sub * j:sub * (j + 1), :] for j in range(n_sub)]
        q_in = jnp.concatenate([qd[j] * jnp.exp(g_start[j]) for j in range(n_sub)], axis=0).astype(BF16)
        k_out = jnp.concatenate([ks[j] * jnp.exp(g_last - g_start[j + 1]) for j in range(n_sub)],
                                axis=0).astype(BF16)
        st = st_ref[hd]
        o_inter = _dot(q_in, st.astype(BF16), _NT)
        scs = []
        for i in range(n_sub):
            qm = jnp.concatenate([qd[j] * jnp.exp(g_start[j] - g_start[i + 1]) for j in range(i, n_sub)],
                                 axis=0).astype(BF16)
            scs.append(_dot(qm, ks[i].astype(BF16), _NT))
        new_st = st * jnp.exp(g_last) + _dot(vb, k_out, _TN)
        return o_inter, scs, vb, new_st

    def head_output(rows, hd, o_h, scs, vb, new_st):
        lo_, hi_ = HGRN_HEAD * hd, HGRN_HEAD * (hd + 1)
        for i, sc in enumerate(scs):
            rr = lax.broadcasted_iota(jnp.int32, sc.shape, 0)
            cc = lax.broadcasted_iota(jnp.int32, sc.shape, 1)
            sc = jnp.where(rr >= cc, sc, 0.0).astype(BF16)
            part = _dot(sc, vb[sub * i:sub * (i + 1), :])
            if i:
                part = jnp.concatenate([jnp.zeros((sub * i, HGRN_HEAD), F32), part], axis=0)
            o_h = o_h + part
        st_ref[hd] = new_st
        oacc_ref[rows, lo_:hi_] = o_h

    strip = 16

    n_chunks = ts // c
    stages = 4
    per_slot = -(-n_pieces // (n_chunks * stages))

    def tile_pass(r_base, h_ref, h_next, x_next_ref, x_next):
        cur.update(h=h_ref, h_next=h_next, x_next=x_next_ref)
        x_next_ref[...] = x_next.astype(BF16)
        pieces = iter(range(n_pieces))

        def project_some():
            for _ in range(per_slot):
                piece = next(pieces, None)
                if piece is not None:
                    project(piece)

        for ci in range(n_chunks):
            r0 = ci * c
            rows = pl.ds(r0, c)
            project_some()
            gates(rows)
            project_some()
            staged = [head_scores(rows, hd) for hd in range(n_heads // 2)]
            project_some()
            staged += [head_scores(rows, hd) for hd in range(n_heads // 2, n_heads)]
            project_some()
            for hd in range(n_heads):
                head_output(rows, hd, *staged[hd])
            for si in range(c // strip):
                rs = pl.ds(r0 + si * strip, strip)
                o = oacc_ref[rs, :]
                g = h_ref[rs, 3 * d:4 * d]
                o = o * lax.rsqrt(jnp.mean(o * o, axis=-1, keepdims=True) + RMS_EPS) * ng_ref[...]
                og_ref[rs, :] = (o * (g * jax.nn.sigmoid(g))).astype(BF16)
        for piece in pieces:
            project(piece)
        mix = _dot(og_ref[...], w_out_ref[...])
        o_ref[0, r_base:r_base + ts, :] = _layer_norm(
            dn_alpha * x_ref[0, r_base:r_base + ts, :] + mix, lg_ref[...], lbias_ref[...])

    tile_pass(0, ha_ref, hb_ref, xa_ref, x_ref[0, ts:2 * ts, :])
    tile_pass(ts, hb_ref, ha_ref, xb_ref, xn_ref[0])


def _next_pair_spec(n_batch, n_pairs, ts, d):
    def index(bi, si):
        nxt = jnp.minimum(bi * n_pairs + si + 1, n_batch * n_pairs - 1)
        return (nxt // n_pairs, 2 * (nxt % n_pairs), 0)
    return pl.BlockSpec((1, ts, d), index)


def _hgrn_layer(x, w_in, lb, norm_g, w_out, ln_g, ln_b, dn_alpha):
    b, s, d = x.shape
    ts = min(SEQ_TILE, s // 2)
    n_pairs = s // (2 * ts)
    vec = pl.BlockSpec((1, d), lambda bi, si: (0, 0))
    return pl.pallas_call(
        functools.partial(_hgrn_kernel, dn_alpha=dn_alpha),
        grid=(b, n_pairs),
        in_specs=[
            pl.BlockSpec((1, 2 * ts, d), lambda bi, si: (bi, si, 0)),
            _next_pair_spec(b, n_pairs, ts, d),
            pl.BlockSpec((d, 4 * d), lambda bi, si: (0, 0)),
            vec, vec,
            pl.BlockSpec((d, d), lambda bi, si: (0, 0)),
            vec, vec,
        ],
        out_specs=pl.BlockSpec((1, 2 * ts, d), lambda bi, si: (bi, si, 0)),
        out_shape=jax.ShapeDtypeStruct((b, s, d), F32),
        scratch_shapes=[
            pltpu.VMEM((ts, 4 * d), F32),
            pltpu.VMEM((ts, 4 * d), F32),
            pltpu.VMEM((ts, d), BF16),
            pltpu.VMEM((ts, d), BF16),
            pltpu.VMEM((ts, d), F32),
            pltpu.VMEM((ts, d), BF16),
            pltpu.VMEM((d // HGRN_HEAD, HGRN_HEAD, HGRN_HEAD), F32),
            pltpu.VMEM((HGRN_CHUNK, d), F32),
            pltpu.VMEM((HGRN_CHUNK, d), F32),
            pltpu.VMEM((HGRN_CHUNK, d), F32),
        ],
        compiler_params=pltpu.CompilerParams(
            dimension_semantics=("arbitrary", "arbitrary"),
            vmem_limit_bytes=V7X_VMEM_LIMIT),
        name="hgrn_mixer",
    )(x, x, w_in, lb, norm_g, w_out, ln_g, ln_b)


def _ret_kernel(x_ref, xn_ref, w_in_ref, cos_ref, sin_e_ref, sin_o_ref, w_out_ref, lg_ref, lbias_ref,
                o_ref, ha_ref, hb_ref, xa_ref, xb_ref, og_ref, st_ref, *, dn_alpha):
    ts, d = xn_ref.shape[1], xn_ref.shape[2]
    qk = d
    vd = 2 * d
    hk = qk // RET_HEADS
    hv = vd // RET_HEADS
    c = RET_CHUNK
    n_pieces = (2 * qk + 2 * vd) // PROJ_PIECE
    n_chunks = ts // c
    stages = 4
    per_slot = -(-n_pieces // (n_chunks * stages))

    @pl.when(pl.program_id(1) == 0)
    def _():
        st_ref[...] = jnp.zeros_like(st_ref)

    @pl.when((pl.program_id(0) == 0) & (pl.program_id(1) == 0))
    def _():
        ha_ref[...] = _dot(x_ref[0, 0:ts, :].astype(BF16), w_in_ref[...])

    ri = lax.broadcasted_iota(jnp.int32, (c, c), 0)
    cj = lax.broadcasted_iota(jnp.int32, (c, c), 1)
    rel = (ri - cj).astype(F32)
    idx_col = lax.broadcasted_iota(jnp.int32, (c, 1), 0).astype(F32)

    def chunk(r_base, ci, h_ref, project_some):
        rows = pl.ds(ci * c, c)
        pos = pl.ds(r_base + ci * c, c)
        project_some()
        cos = jnp.tile(cos_ref[pos, :], (1, RET_HEADS))
        sin_e = jnp.tile(sin_e_ref[pos, :], (1, RET_HEADS))
        sin_o = jnp.tile(sin_o_ref[pos, :], (1, RET_HEADS))

        def rope(t):
            nxt = pltpu.roll(t, qk - 1, 1)
            prv = pltpu.roll(t, 1, 1)
            return t * cos + nxt * sin_e + prv * sin_o

        q = rope(h_ref[rows, 0:qk]).astype(BF16)
        k = rope(h_ref[rows, qk:2 * qk]) * (hk ** -0.5)
        v = h_ref[rows, 2 * qk:2 * qk + vd].astype(BF16)
        lgam = [math.log(1.0 - 2.0 ** (-5.0 - hd)) for hd in range(RET_HEADS)]
        project_some()
        staged = []
        for hd in range(RET_HEADS):
            zeta = jnp.exp((c - 1.0 - idx_col) * lgam[hd])
            q_h = q[:, hk * hd:hk * (hd + 1)]
            k_h = k[:, hk * hd:hk * (hd + 1)]
            v_h = v[:, hv * hd:hv * (hd + 1)]
            st = st_ref[hd]
            sc = _dot(q_h, k_h.astype(BF16), _NT)
            o_inter = _dot(q_h, st.astype(BF16))
            new_st = st * math.exp(c * lgam[hd]) + _dot((k_h * zeta).astype(BF16), v_h, _TN)
            staged.append((sc, o_inter, new_st, v_h))
            if hd == RET_HEADS // 2 - 1:
                project_some()
        project_some()
        for hd, (sc, o_inter, new_st, v_h) in enumerate(staged):
            decay = jnp.where(rel >= 0, jnp.exp(rel * lgam[hd]), 0.0)
            xi = jnp.exp((idx_col + 1.0) * lgam[hd])
            o_h = _dot((sc * decay).astype(BF16), v_h) + o_inter * xi
            st_ref[hd] = new_st
            mu = jnp.mean(o_h, axis=-1, keepdims=True)
            dlt = o_h - mu
            var = jnp.mean(dlt * dlt, axis=-1, keepdims=True)
            g_h = h_ref[rows, 2 * qk + vd + hv * hd:2 * qk + vd + hv * (hd + 1)]
            og_ref[rows, hv * hd:hv * (hd + 1)] = (
                (g_h * jax.nn.sigmoid(g_h)) * (dlt * lax.rsqrt(var + LN_EPS))).astype(BF16)

    def tile_pass(r_base, h_ref, h_next, x_next_ref, x_next):
        x_next_ref[...] = x_next.astype(BF16)
        pieces = iter(range(n_pieces))

        def project(piece):
            cols = slice(PROJ_PIECE * piece, PROJ_PIECE * (piece + 1))
            h_next[:, cols] = _dot(x_next_ref[...], w_in_ref[:, cols])

        def project_some():
            for _ in range(per_slot):
                piece = next(pieces, None)
                if piece is not None:
                    project(piece)

        for ci in range(n_chunks):
            chunk(r_base, ci, h_ref, project_some)
        for piece in pieces:
            project(piece)
        mix = _dot(og_ref[...], w_out_ref[...])
        o_ref[0, r_base:r_base + ts, :] = _layer_norm(
            dn_alpha * x_ref[0, r_base:r_base + ts, :] + mix, lg_ref[...], lbias_ref[...])

    tile_pass(0, ha_ref, hb_ref, xa_ref, x_ref[0, ts:2 * ts, :])
    tile_pass(ts, hb_ref, ha_ref, xb_ref, xn_ref[0])


def _ret_layer(x, w_in, cos, sin_e, sin_o, w_out, ln_g, ln_b, dn_alpha):
    b, s, d = x.shape
    ts = min(SEQ_TILE, s // 2)
    n_pairs = s // (2 * ts)
    n_in = w_in.shape[1]
    hk = d // RET_HEADS
    vec = pl.BlockSpec((1, d), lambda bi, si: (0, 0))
    tab = pl.BlockSpec((2 * ts, hk), lambda bi, si: (si, 0))
    once = pl.Buffered(1)
    return pl.pallas_call(
        functools.partial(_ret_kernel, dn_alpha=dn_alpha),
        grid=(b, n_pairs),
        in_specs=[
            pl.BlockSpec((1, 2 * ts, d), lambda bi, si: (bi, si, 0)),
            _next_pair_spec(b, n_pairs, ts, d),
            pl.BlockSpec((d, n_in), lambda bi, si: (0, 0), pipeline_mode=once),
            tab, tab, tab,
            pl.BlockSpec((2 * d, d), lambda bi, si: (0, 0), pipeline_mode=once),
            vec, vec,
        ],
        out_specs=pl.BlockSpec((1, 2 * ts, d), lambda bi, si: (bi, si, 0)),
        out_shape=jax.ShapeDtypeStruct((b, s, d), F32),
        scratch_shapes=[
            pltpu.VMEM((ts, n_in), F32),
            pltpu.VMEM((ts, n_in), F32),
            pltpu.VMEM((ts, d), BF16),
            pltpu.VMEM((ts, d), BF16),
            pltpu.VMEM((ts, 2 * d), BF16),
            pltpu.VMEM((RET_HEADS, hk, 2 * d // RET_HEADS), F32),
        ],
        compiler_params=pltpu.CompilerParams(
            dimension_semantics=("arbitrary", "arbitrary"),
            vmem_limit_bytes=V7X_VMEM_LIMIT),
        name="ret_mixer",
    )(x, x, w_in, cos, sin_e, sin_o, w_out, ln_g, ln_b)


def _rotary_tables(seq_len, hk):
    pos = jnp.arange(seq_len, dtype=F32)
    inv = 1.0 / (ROPE_BASE ** jnp.linspace(0.0, 1.0, hk // 2, dtype=F32))
    theta = pos[:, None] * jnp.repeat(inv, 2)[None, :]
    cos, sin = jnp.cos(theta), jnp.sin(theta)
    even = (jnp.arange(hk) % 2 == 0)[None, :]
    return cos, jnp.where(even, -sin, 0.0), jnp.where(even, 0.0, sin)


def _router_kernel(x_ref, w_ref, b_ref, idx_ref, gate_ref, rank_ref, cnt_ref):
    t = x_ref.shape[0]
    e = w_ref.shape[0]

    xh, xm, xl = _split3(x_ref[...])
    wh, wm, wl = _split3(w_ref[...])
    logits = (_dot(wl, xh, _NT) + _dot(wh, xl, _NT) + _dot(wm, xm, _NT)
              + _dot(wm, xh, _NT) + _dot(wh, xm, _NT) + _dot(wh, xh, _NT)) + b_ref[...]

    eid = lax.broadcasted_iota(jnp.int32, (e, t), 0)
    ti = lax.broadcasted_iota(jnp.int32, (t, t), 0)
    tj = lax.broadcasted_iota(jnp.int32, (t, t), 1)
    before = jnp.where(ti < tj, 1.0, 0.0).astype(BF16)
    work = logits
    vals, idxs, ranks = [], [], []
    base = jnp.zeros((e, 1), F32)
    for _ in range(TOP_K):
        m = jnp.max(work, axis=0, keepdims=True)
        sel = jnp.min(jnp.where(work == m, eid, e), axis=0, keepdims=True)
        hit = eid == sel
        work = jnp.where(hit, -jnp.inf, work)
        onehot = jnp.where(hit, 1.0, 0.0)
        excl = _dot(onehot.astype(BF16), before)
        rank = jnp.sum(onehot * (excl + base), axis=0, keepdims=True)
        base = base + jnp.sum(onehot, axis=1, keepdims=True)
        vals.append(m)
        idxs.append(sel)
        ranks.append(rank)
    cnt_ref[0] = base.astype(jnp.int32)
    ex = [jnp.exp(vv - vals[0]) for vv in vals]
    den = ex[0] + ex[1] + ex[2] + ex[3]
    gate_ref[...] = jnp.concatenate(ex, axis=0) / den
    idx_ref[...] = jnp.concatenate(idxs, axis=0)
    rank_ref[...] = jnp.concatenate(ranks, axis=0).astype(jnp.int32)


def _router(x2, w_t, b_col):
    n, d = x2.shape
    e = w_t.shape[0]
    t = min(ROUTER_TILE, n)
    kt = pl.BlockSpec((TOP_K, t), lambda i: (0, i))
    return pl.pallas_call(
        _router_kernel,
        grid=(n // t,),
        in_specs=[
            pl.BlockSpec((t, d), lambda i: (i, 0)),
            pl.BlockSpec((e, d), lambda i: (0, 0)),
            pl.BlockSpec((e, 1), lambda i: (0, 0)),
        ],
        out_specs=[kt, kt, kt, pl.BlockSpec((1, e, 1), lambda i: (i, 0, 0))],
        out_shape=[
            jax.ShapeDtypeStruct((TOP_K, n), jnp.int32),
            jax.ShapeDtypeStruct((TOP_K, n), F32),
            jax.ShapeDtypeStruct((TOP_K, n), jnp.int32),
            jax.ShapeDtypeStruct((n // t, e, 1), jnp.int32),
        ],
        compiler_params=pltpu.CompilerParams(dimension_semantics=("arbitrary",)),
        name="moe_router",
    )(x2, w_t, b_col)


def _plan_kernel(cnt_ref, idx_ref, rank_ref, lpos_ref, tab_ref, fill_ref, acc_ref):
    i = pl.program_id(0)
    nt, e, _ = cnt_ref.shape
    t = idx_ref.shape[1]
    g_shift = ROW_GRANULE.bit_length() - 1
    b_shift = (EXPERT_ROWS // ROW_GRANULE).bit_length() - 1

    def units(cnt):
        return (cnt + (ROW_GRANULE - 1)) >> g_shift

    @pl.when(i == 0)
    def _():
        acc_ref[...] = jnp.zeros_like(acc_ref)

    total = units(cnt_ref[0])
    for j in range(1, nt):
        total = total + units(cnt_ref[j])
    pad_col = (((total + ((1 << b_shift) - 1)) >> b_shift) << b_shift).astype(F32)
    r = lax.broadcasted_iota(jnp.int32, (e, e), 0)
    c = lax.broadcasted_iota(jnp.int32, (e, e), 1)

    def to_row(col):
        return jnp.sum(jnp.where(r == c, col, 0.0), axis=0, keepdims=True)

    def prefix(col):
        return jnp.sum(jnp.where(c < r, to_row(col), 0.0), axis=1, keepdims=True)

    start_col = prefix(pad_col)
    end_col = start_col + pad_col
    used_col = start_col + total.astype(F32)

    mine = units(cnt_ref[i]).astype(F32)
    local_col = prefix(mine)
    global_col = start_col + acc_ref[...]
    acc_ref[...] = acc_ref[...] + mine
    tab_ref[0, 0:1, :] = to_row(local_col).astype(jnp.int32)
    tab_ref[0, 1:2, :] = to_row(mine).astype(jnp.int32)
    tab_ref[0, 2:3, :] = to_row(global_col).astype(jnp.int32)

    lane =lax.broadcasted_iota(jnp.int32, (e, fill_ref.shape[1]), 1)
    sub = lax.broadcasted_iota(jnp.int32, (e, fill_ref.shape[1]), 0)
    fill = jnp.where(lane == sub, used_col, 0.0) + jnp.where(lane == sub + e, end_col, 0.0)
    fill_ref[...] = jnp.sum(fill, axis=0, keepdims=True).astype(jnp.int32) * ROW_GRANULE

    eid = lax.broadcasted_iota(jnp.int32, (e, t), 0)
    for kk in range(TOP_K):
        hit = eid == idx_ref[kk:kk + 1, :]
        rank = rank_ref[kk:kk + 1, :]
        l_k = jnp.sum(jnp.where(hit, local_col, 0.0), axis=0, keepdims=True).astype(jnp.int32)
        lpos_ref[kk:kk + 1, :] = l_k * ROW_GRANULE + rank


def _plan(counts, idx, rank):
    nt, e, _ = counts.shape
    n = idx.shape[1]
    t = n // nt
    kt = pl.BlockSpec((TOP_K, t), lambda i: (0, i))
    fill_lanes = V7X_LANES * (-(-2 * e // V7X_LANES))
    return pl.pallas_call(
        _plan_kernel,
        grid=(nt,),
        in_specs=[pl.BlockSpec((nt, e, 1), lambda i: (0, 0, 0)), kt, kt],
        out_specs=[
            kt,
            pl.BlockSpec((1, 3, e), lambda i: (i, 0, 0)),
            pl.BlockSpec((1, fill_lanes), lambda i: (0, 0)),
        ],
        out_shape=[
            jax.ShapeDtypeStruct((TOP_K, n), jnp.int32),
            jax.ShapeDtypeStruct((nt, 3, e), jnp.int32),
            jax.ShapeDtypeStruct((1, fill_lanes), jnp.int32),
        ],
        scratch_shapes=[pltpu.VMEM((e, 1), F32)],
        compiler_params=pltpu.CompilerParams(dimension_semantics=("arbitrary",)),
        name="moe_plan",
    )(counts, idx, rank)


def _slab_copies(fn, tab_ref, step, n_experts, local_ref, slot, sorted_ref, sem, to_sorted):
    def per_expert(ee, carry):
        base = step * (3 * n_experts)
        src_u = tab_ref[base + ee]
        n_u = tab_ref[base + n_experts + ee]
        dst_u = tab_ref[base + 2 * n_experts + ee]
        for b in range(SLAB_BITS):
            @pl.when(((n_u >> b) & 1) == 1)
            def _():
                off = n_u & ((1 << b) - 1)
                rows = ROW_GRANULE << b
                loc = local_ref.at[slot, pl.ds(pl.multiple_of((src_u + off) * ROW_GRANULE, ROW_GRANULE), rows)]
                glob = sorted_ref.at[pl.ds(pl.multiple_of((dst_u + off) * ROW_GRANULE, ROW_GRANULE), rows)]
                fn(pltpu.make_async_copy(loc, glob, sem.at[slot]) if to_sorted
                   else pltpu.make_async_copy(glob, loc, sem.at[slot]))
        return carry

    lax.fori_loop(0, n_experts, per_expert, 0)


def _dispatch_kernel(tab_ref, fill_ref, x_ref, lpos_ref, xs_out, xloc_ref, zero_ref, sem_run, sem_zero,
                     *, n_experts):
    t = x_ref.shape[0]
    n_rows = xs_out.shape[0]
    local_rows = xloc_ref.shape[1]
    i = pl.program_id(0)
    slot = i % 2

    @pl.when(i == 0)
    def _():
        zero_ref[...] = jnp.zeros_like(zero_ref)
        tail_blk = fill_ref[2 * n_experts - 1] // EXPERT_ROWS

        def fill_copies(fn):
            def per_expert(ee, carry):
                def one(uu, c2):
                    r0 = pl.multiple_of(uu * ROW_GRANULE, ROW_GRANULE)
                    fn(pltpu.make_async_copy(zero_ref.at[pl.ds(0, ROW_GRANULE)],
                                             xs_out.at[pl.ds(r0, ROW_GRANULE)], sem_zero))
                    return c2
                lax.fori_loop(fill_ref[ee] // ROW_GRANULE, fill_ref[n_experts + ee] // ROW_GRANULE, one, 0)
                return carry
            lax.fori_loop(0, n_experts, per_expert, 0)

            def per_tail(bi, carry):
                r0 = pl.multiple_of(bi * EXPERT_ROWS, EXPERT_ROWS)
                fn(pltpu.make_async_copy(zero_ref, xs_out.at[pl.ds(r0, EXPERT_ROWS)], sem_zero))
                return carry
            lax.fori_loop(tail_blk, n_rows // EXPERT_ROWS, per_tail, 0)

        fill_copies(lambda cp: cp.start())
        fill_copies(lambda cp: cp.wait())

    xb = x_ref[...].astype(BF16)
    lp = lpos_ref[...]
    for rc in range(local_rows // SORT_CHUNK):
        rows = lax.broadcasted_iota(jnp.int32, (SORT_CHUNK, t), 0) + rc * SORT_CHUNK
        sel = jnp.zeros((SORT_CHUNK, t), F32)
        for kk in range(TOP_K):
            sel = jnp.where(rows == lp[kk:kk + 1, :], 1.0, sel)
        xloc_ref[slot, SORT_CHUNK * rc:SORT_CHUNK * (rc + 1), :] = _dot(sel.astype(BF16), xb)

    move = functools.partial(_slab_copies, tab_ref=tab_ref, n_experts=n_experts, local_ref=xloc_ref,
                             sorted_ref=xs_out, sem=sem_run, to_sorted=True)
    move(lambda cp: cp.start(), step=i, slot=slot)

    @pl.when(i > 0)
    def _():
        move(lambda cp: cp.wait(), step=i - 1, slot=1 - slot)

    @pl.when(i == pl.num_programs(0) - 1)
    def _():
        move(lambda cp: cp.wait(), step=i, slot=slot)


def _dispatch(x2, lpos, tab, fill, n_rows, n_experts):
    n, d = x2.shape
    nt = tab.shape[0] // (3 * n_experts)
    t = n // nt
    local_rows = -(-(TOP_K * t + (ROW_GRANULE - 1) * n_experts) // SORT_CHUNK) * SORT_CHUNK
    grid_spec = pltpu.PrefetchScalarGridSpec(
        num_scalar_prefetch=2,
        grid=(nt,),
        in_specs=[
            pl.BlockSpec((t, d), lambda i, tb, fl: (i, 0)),
            pl.BlockSpec((TOP_K, t), lambda i, tb, fl: (0, i)),
        ],
        out_specs=pl.BlockSpec(memory_space=pl.ANY),
        scratch_shapes=[
            pltpu.VMEM((2, local_rows, d), F32),
            pltpu.VMEM((EXPERT_ROWS, d), F32),
            pltpu.SemaphoreType.DMA((2,)),
            pltpu.SemaphoreType.DMA,
        ],
    )
    return pl.pallas_call(
        functools.partial(_dispatch_kernel, n_experts=n_experts),
        grid_spec=grid_spec,
        out_shape=jax.ShapeDtypeStruct((n_rows, d), F32),
        compiler_params=pltpu.CompilerParams(
            dimension_semantics=("arbitrary",), vmem_limit_bytes=V7X_VMEM_LIMIT),
        name="moe_dispatch",
    )(tab, fill, x2, lpos)


def _expert_kernel(fill_ref, xs_hbm, wgu_ref, bg_ref, bl_ref, wd_ref, bd_ref, y_hbm,
                   xbuf, ybuf, wt_ref, wg_s, wl_s, wd_s, sem_in, sem_out, sem_zero, *, n_experts):
    e = pl.program_id(0)
    n_rows = y_hbm.shape[0]
    ff, d = wg_s.shape
    tc, rk = wt_ref.shape
    half = tc // 2
    run_start = jnp.where(e == 0, 0, fill_ref[n_experts + jnp.maximum(e - 1, 0)])
    run_end = fill_ref[n_experts + e]
    n_blk = (run_end - run_start) // EXPERT_ROWS

    def rows(j):
        return pl.ds(pl.multiple_of(run_start + j * EXPERT_ROWS, EXPERT_ROWS), EXPERT_ROWS)

    def load(j):
        return pltpu.make_async_copy(xs_hbm.at[rows(j)], xbuf.at[j % 2], sem_in.at[j % 2])

    def store(j):
        return pltpu.make_async_copy(ybuf.at[j % 2], y_hbm.at[rows(j)], sem_out.at[j % 2])

    @pl.when(n_blk > 0)
    def _():
        load(0).start()
        for kb in range(d // rk):
            for j in range(2 * ff // tc):
                wt_ref[...] = wgu_ref[0, 0, rk * kb:rk * (kb + 1), tc * j:tc * (j + 1)].T
                wg_s[half * j:half * (j + 1), rk * kb:rk * (kb + 1)] = (
                    wt_ref[pl.ds(0, half, stride=2), :].astype(BF16))
                wl_s[half * j:half * (j + 1), rk * kb:rk * (kb + 1)] = (
                    wt_ref[pl.ds(1, half, stride=2), :].astype(BF16))
        wd_s[...] = wd_ref[0, 0].astype(BF16)

        def block(j, carry):
            slot = j % 2
            load(j).wait()

            @pl.when(j + 1 < n_blk)
            def _():
                load(j + 1).start()

            @pl.when(j >= 2)
            def _():
                store(j - 2).wait()

            xb = xbuf[slot].astype(BF16)
            hg = _dot(xb, wg_s[...], _NT) + bg_ref[0]
            hl = _dot(xb, wl_s[...], _NT) + bl_ref[0]
            glu = jnp.minimum(hg, SWIGLU_LIMIT)
            lin = jnp.clip(hl, -SWIGLU_LIMIT, SWIGLU_LIMIT)
            a = glu * jax.nn.sigmoid(SWIGLU_ALPHA * glu) * (lin + 1.0)
            ybuf[slot] = _dot(a.astype(BF16), wd_s[...]) + bd_ref[0, 0]
            store(j).start()
            return carry

        lax.fori_loop(0, n_blk, block, 0)

        @pl.when(n_blk >= 2)
        def _():
            store(n_blk - 2).wait()

        store(n_blk - 1).wait()

    @pl.when(e == n_experts - 1)
    def _():
        xbuf[0] = jnp.zeros((EXPERT_ROWS, d), F32)

        def tail(fn):
            def one(bi, carry):
                r0 = pl.multiple_of(bi * EXPERT_ROWS, EXPERT_ROWS)
                fn(pltpu.make_async_copy(xbuf.at[0], y_hbm.at[pl.ds(r0, EXPERT_ROWS)], sem_zero))
                return carry
            lax.fori_loop(run_end // EXPERT_ROWS, n_rows // EXPERT_ROWS, one, 0)

        tail(lambda cp: cp.start())
        tail(lambda cp: cp.wait())


def _experts(xs, fill, w_gu, b_glu, b_lin, w_dn, b_dn, layer):
    n_rows, d = xs.shape
    ff = w_dn.shape[2]
    e = w_dn.shape[1]
    wspec = lambda r, c: pl.BlockSpec((1, r, c), lambda i, fl: (i, 0, 0))
    lspec = lambda r, c: pl.BlockSpec((1, 1, r, c), lambda i, fl: (layer, i, 0, 0))
    grid_spec = pltpu.PrefetchScalarGridSpec(
        num_scalar_prefetch=1,
        grid=(e,),
        in_specs=[
            pl.BlockSpec(memory_space=pl.ANY),
            lspec(d, 2 * ff), wspec(1, ff), wspec(1, ff),
            lspec(ff, d), lspec(1, d),
        ],
        out_specs=pl.BlockSpec(memory_space=pl.ANY),
        scratch_shapes=[
            pltpu.VMEM((2, EXPERT_ROWS, d), F32),
            pltpu.VMEM((2, EXPERT_ROWS, d), F32),
            pltpu.VMEM((WEIGHT_SPLIT_COLS, V7X_LANES), F32),
            pltpu.VMEM((ff, d), BF16),
            pltpu.VMEM((ff, d), BF16),
            pltpu.VMEM((ff, d), BF16),
            pltpu.SemaphoreType.DMA((2,)),
            pltpu.SemaphoreType.DMA((2,)),
            pltpu.SemaphoreType.DMA,
        ],
    )
    return pl.pallas_call(
        functools.partial(_expert_kernel, n_experts=e),
        grid_spec=grid_spec,
        out_shape=jax.ShapeDtypeStruct((n_rows, d), F32),
        compiler_params=pltpu.CompilerParams(
            dimension_semantics=("arbitrary",), vmem_limit_bytes=V7X_VMEM_LIMIT),
        name="moe_experts",
    )(fill, xs, w_gu, b_glu, b_lin, w_dn, b_dn)


def _combine_kernel(tab_ref, y_hbm, x_ref, lpos_ref, gate_ref, lg_ref, lbias_ref, o_ref,
                    yloc_ref, sem_run, *, dn_alpha, n_experts):
    t = x_ref.shape[0]
    local_rows = yloc_ref.shape[1]
    i = pl.program_id(0)
    slot = i % 2
    fetch = functools.partial(_slab_copies, tab_ref=tab_ref, n_experts=n_experts, local_ref=yloc_ref,
                              sorted_ref=y_hbm, sem=sem_run, to_sorted=False)

    @pl.when(i == 0)
    def _():
        yloc_ref[...] = jnp.zeros_like(yloc_ref)
        fetch(lambda cp: cp.start(), step=0, slot=0)

    @pl.when(i + 1 < pl.num_programs(0))
    def _():
        fetch(lambda cp: cp.start(), step=i + 1, slot=1 - slot)

    fetch(lambda cp: cp.wait(), step=i, slot=slot)

    lp = lpos_ref[...]
    gate = gate_ref[...]
    ffn = jnp.zeros(x_ref.shape, F32)
    for rc in range(local_rows // SORT_CHUNK):
        cols = lax.broadcasted_iota(jnp.int32, (t, SORT_CHUNK), 1) + rc * SORT_CHUNK
        w = jnp.zeros((t, SORT_CHUNK), F32)
        for kk in range(TOP_K):
            w = jnp.where(cols == lp[:, kk:kk + 1], gate[:, kk:kk + 1], w)
        y_c = yloc_ref[slot, SORT_CHUNK * rc:SORT_CHUNK * (rc + 1), :].astype(BF16)
        ffn = ffn + _dot(w.astype(BF16), y_c)
    o_ref[...] = _layer_norm(dn_alpha * x_ref[...] + ffn, lg_ref[...], lbias_ref[...])


def _combine(y, x2, lpos_t, gate_t, tab, ln_g, ln_b, dn_alpha, n_experts):
    n, d = x2.shape
    nt = tab.shape[0] // (3 * n_experts)
    t = n // nt
    local_rows = -(-(TOP_K * t + (ROW_GRANULE - 1) * n_experts) // SORT_CHUNK) * SORT_CHUNK
    vec = pl.BlockSpec((1, d), lambda i, tb: (0, 0))
    tk = pl.BlockSpec((t, TOP_K), lambda i, tb: (i, 0))
    grid_spec = pltpu.PrefetchScalarGridSpec(
        num_scalar_prefetch=1,
        grid=(nt,),
        in_specs=[
            pl.BlockSpec(memory_space=pl.ANY),
            pl.BlockSpec((t, d), lambda i, tb: (i, 0)),
            tk, tk, vec, vec,
        ],
        out_specs=pl.BlockSpec((t, d), lambda i, tb: (i, 0)),
        scratch_shapes=[
            pltpu.VMEM((2, local_rows, d), F32),
            pltpu.SemaphoreType.DMA((2,)),
        ],
    )
    return pl.pallas_call(
        functools.partial(_combine_kernel, dn_alpha=dn_alpha, n_experts=n_experts),
        grid_spec=grid_spec,
        out_shape=jax.ShapeDtypeStruct((n, d), F32),
        compiler_params=pltpu.CompilerParams(
            dimension_semantics=("arbitrary",), vmem_limit_bytes=V7X_VMEM_LIMIT),
        name="moe_combine",
    )(tab, y, x2, lpos_t, gate_t, ln_g, ln_b)


def _moe_layer(x2, w_r, b_r, w_gu, b_glu, b_lin, w_dn, b_dn, layer, ln_g, ln_b, dn_alpha):
    n, d = x2.shape
    e = w_r.shape[1]
    idx, gate, rank, counts = _router(x2, w_r.T, b_r.reshape(e, 1))
    n_blocks = -(-(n * TOP_K + (ROW_GRANULE - 1) * counts.shape[0] * e) // EXPERT_ROWS) + e
    lpos, tab, fill = _plan(counts, idx, rank)
    fill = fill.reshape(-1)
    tab = tab.reshape(-1)
    xs = _dispatch(x2, lpos, tab, fill, n_blocks * EXPERT_ROWS, e)
    y = _experts(xs, fill, w_gu, b_glu, b_lin, w_dn, b_dn, layer)
    return _combine(y, x2, lpos.T, gate.T, tab, ln_g, ln_b, dn_alpha, e)


def kernel(x, hgrn_w_in, hgrn_norm_g, hgrn_w_out, lower_bounds, ret_w_in, ret_w_out, ln_mix_g, ln_mix_b,
           ln_ffn_g, ln_ffn_b, router_w, router_b, expert_w_gu, expert_b_gu, expert_w_down, expert_b_down):
    b, s, d = x.shape
    depth = lower_bounds.shape[0]
    dn_alpha = (2 * depth) ** 0.25
    lb_soft = jax.nn.softmax(lower_bounds.astype(F32), axis=0)
    lbs = jnp.cumsum(lb_soft, axis=0) - lb_soft[0]
    cos, sin_e, sin_o = _rotary_tables(s, d // RET_HEADS)
    row = lambda p, i: p[i].reshape(1, -1)
    for i in range(depth):
        j = i // 2
        if i % 2 == 0:
            x = _hgrn_layer(x, hgrn_w_in[j].astype(BF16), row(lbs, i), row(hgrn_norm_g, j),
                            hgrn_w_out[j].astype(BF16), row(ln_mix_g, i), row(ln_mix_b, i), dn_alpha)
        else:
            x = _ret_layer(x, ret_w_in[j].astype(BF16), cos, sin_e, sin_o, ret_w_out[j].astype(BF16),
                           row(ln_mix_g, i), row(ln_mix_b, i), dn_alpha)
        b_gu = expert_b_gu[i]
        x2 = _moe_layer(
            x.reshape(b * s, d), router_w[i], router_b[i], expert_w_gu,
            b_gu[:, None, 0::2], b_gu[:, None, 1::2],
            expert_w_down, expert_b_down[:, :, None, :], i,
            row(ln_ffn_g, i), row(ln_ffn_b, i), dn_alpha)
        x = x2.reshape(b, s, d)
    return x
```

```python
import functools
import math

import jax
import jax.numpy as jnp
from jax import lax
from jax.experimental import pallas as pl
from jax.experimental.pallas import tpu as pltpu

HGRN_HEAD = 128
RET_HEADS = 4
ROPE_BASE = 10000.0
TOP_K = 4
SWIGLU_LIMIT = 7.0
SWIGLU_ALPHA = 1.702
LN_EPS = 1e-5
RMS_EPS = 1e-6

V7X_VMEM_LIMIT = 56 * 1024 * 1024
V7X_LANES = 128
SEQ_TILE = 256
HGRN_CHUNK = 64
HGRN_SUB = 16
RET_CHUNK = 128
ROUTER_TILE = 512
EXPERT_ROWS = 512
ROW_GRANULE = 8
SLAB_BITS = (ROUTER_TILE // ROW_GRANULE).bit_length()
SORT_CHUNK = 256
PROJ_PIECE = 256
WEIGHT_SPLIT_COLS = 256

F32 = jnp.float32
BF16 = jnp.bfloat16

_NT = (((1,), (1,)), ((), ()))
_TN = (((0,), (0,)), ((), ()))


def _dot(a, b, dims=None):
    if dims is None:
        return jnp.dot(a, b, preferred_element_type=F32)
    return lax.dot_general(a, b, dims, preferred_element_type=F32)


def _layer_norm(v, g, b):
    mu = jnp.mean(v, axis=-1, keepdims=True)
    d = v - mu
    var = jnp.mean(d * d, axis=-1, keepdims=True)
    return d * lax.rsqrt(var + LN_EPS) * g + b


def _split3(v):
    hi = v.astype(BF16)
    r1 = v - hi.astype(F32)
    mid = r1.astype(BF16)
    lo = (r1 - mid.astype(F32)).astype(BF16)
    return hi, mid, lo


def _hgrn_kernel(x_ref, xn_ref, w_in_ref, lb_ref, ng_ref, w_out_ref, lg_ref, lbias_ref,
                 o_ref, ha_ref, hb_ref, xa_ref, xb_ref, oacc_ref, og_ref, st_ref, qa_ref, k_ref, b_ref,
                 *, dn_alpha):
    ts, d = xn_ref.shape[1], xn_ref.shape[2]
    n_heads = d // HGRN_HEAD
    c = HGRN_CHUNK
    n_sub = c // HGRN_SUB
    cur = {}

    @pl.when(pl.program_id(1) == 0)
    def _():
        st_ref[...] = jnp.zeros_like(st_ref)

    @pl.when((pl.program_id(0) == 0) & (pl.program_id(1) == 0))
    def _():
        ha_ref[...] = _dot(x_ref[0, 0:ts, :].astype(BF16), w_in_ref[...])

    n_pieces = 4 * d // PROJ_PIECE

    def project(piece):
        cols = slice(PROJ_PIECE * piece, PROJ_PIECE * (piece + 1))
        cur["h_next"][:, cols] = _dot(cur["x_next"][...], w_in_ref[:, cols])

    sub = HGRN_SUB
    row = lax.broadcasted_iota(jnp.int32, (c, c), 0)
    col = lax.broadcasted_iota(jnp.int32, (c, c), 1)
    sub_shift = sub.bit_length() - 1
    same_sub = (row >> sub_shift) == (col >> sub_shift)
    tril = jnp.where(same_sub & (row >= col), 1.0, 0.0).astype(BF16)

    def gates(rows):
        lb = lb_ref[...]
        one_m_lb = 1.0 - lb
        q = cur["h"][rows, 0:d]
        sf = cur["h"][rows, d:2 * d]
        qa_ref[...] = q * jax.nn.sigmoid(q)
        k_ref[...] = one_m_lb * jax.nn.sigmoid(-sf)
        logf = jnp.log(lb + one_m_lb * jax.nn.sigmoid(sf))
        hi = logf.astype(BF16)
        lo = (logf - hi.astype(F32)).astype(BF16)
        b_ref[...] = _dot(tril, lo) + _dot(tril, hi)

    def head_scores(rows, hd):
        lo_, hi_ = HGRN_HEAD * hd, HGRN_HEAD * (hd + 1)
        b = b_ref[:, lo_:hi_]
        vb = cur["h"][rows, 2 * d + lo_:2 * d + hi_].astype(BF16)
        tot = [b[sub * j + sub - 1:sub * j + sub, :] for j in range(n_sub)]
        to_end = jnp.concatenate([jnp.broadcast_to(t, (sub, HGRN_HEAD)) for t in tot], axis=0) - b
        q_dec = qa_ref[:, lo_:hi_] * jnp.exp(b)
        k_st = k_ref[:, lo_:hi_] * jnp.exp(to_end)
        g_start = [jnp.zeros_like(tot[0])]
        for j in range(n_sub):
            g_start.append(g_start[j] + tot[j])
        g_last = g_start[n_sub]
        qd = [q_dec[sub * j:sub * (j + 1), :] for j in range(n_sub)]
        ks = [k_st[sub * j:sub * (j + 1), :] for j in range(n_sub)]
        q_in = jnp.concatenate([qd[j] * jnp.exp(g_start[j]) for j in range(n_sub)], axis=0).astype(BF16)
        k_out = jnp.concatenate([ks[j] * jnp.exp(g_last - g_start[j + 1]) for j in range(n_sub)],
                                axis=0).astype(BF16)
        st = st_ref[hd]
        o_inter = _dot(q_in, st.astype(BF16), _NT)
        scs = []
        for i in range(n_sub):
            qm = jnp.concatenate([qd[j] * jnp.exp(g_start[j] - g_start[i + 1]) for j in range(i, n_sub)],
                                 axis=0).astype(BF16)
            scs.append(_dot(qm, ks[i].astype(BF16), _NT))
        new_st = st * jnp.exp(g_last) + _dot(vb, k_out, _TN)
        return o_inter, scs, vb, new_st

    def head_output(rows, hd, o_h, scs, vb, new_st):
        lo_, hi_ = HGRN_HEAD * hd, HGRN_HEAD * (hd + 1)
        for i, sc in enumerate(scs):
            rr = lax.broadcasted_iota(jnp.int32, sc.shape, 0)
            cc = lax.broadcasted_iota(jnp.int32, sc.shape, 1)
            sc = jnp.where(rr >= cc, sc, 0.0).astype(BF16)
            part = _dot(sc, vb[sub * i:sub * (i + 1), :])
            if i:
                part = jnp.concatenate([jnp.zeros((sub * i, HGRN_HEAD), F32), part], axis=0)
            o_h = o_h + part
        st_ref[hd] = new_st
        oacc_ref[rows, lo_:hi_] = o_h

    strip = 16

    n_chunks = ts // c
    stages = 4
    per_slot = -(-n_pieces // (n_chunks * stages))

    def tile_pass(r_base, h_ref, h_next, x_next_ref, x_next):
        cur.update(h=h_ref, h_next=h_next, x_next=x_next_ref)
        x_next_ref[...] = x_next.astype(BF16)
        pieces = iter(range(n_pieces))

        def project_some():
            for _ in range(per_slot):
                piece = next(pieces, None)
                if piece is not None:
                    project(piece)

        for ci in range(n_chunks):
            r0 = ci * c
            rows = pl.ds(r0, c)
            project_some()
            gates(rows)
            project_some()
            staged = [head_scores(rows, hd) for hd in range(n_heads // 2)]
            project_some()
            staged += [head_scores(rows, hd) for hd in range(n_heads // 2, n_heads)]
            project_some()
            for hd in range(n_heads):
                head_output(rows, hd, *staged[hd])
            for si in range(c // strip):
                rs = pl.ds(r0 + si * strip, strip)
                o = oacc_ref[rs, :]
                g = h_ref[rs, 3 * d:4 * d]
                o = o * lax.rsqrt(jnp.mean(o * o, axis=-1, keepdims=True) + RMS_EPS) * ng_ref[...]
                og_ref[rs, :] = (o * (g * jax.nn.sigmoid(g))).astype(BF16)
        for piece in pieces:
            project(piece)
        mix = _dot(og_ref[...], w_out_ref[...])
        o_ref[0, r_base:r_base + ts, :] = _layer_norm(
            dn_alpha * x_ref[0, r_base:r_base + ts, :] + mix, lg_ref[...], lbias_ref[...])

    tile_pass(0, ha_ref, hb_ref, xa_ref, x_ref[0, ts:2 * ts, :])
    tile_pass(ts, hb_ref, ha_ref, xb_ref, xn_ref[0])


def _next_pair_spec(n_batch, n_pairs, ts, d):
    def index(bi, si):
        nxt = jnp.minimum(bi * n_pairs + si + 1, n_batch * n_pairs - 1)
        return (nxt // n_pairs, 2 * (nxt % n_pairs), 0)
    return pl.BlockSpec((1, ts, d), index)


def _hgrn_layer(x, w_in, lb, norm_g, w_out, ln_g, ln_b, dn_alpha):
    b, s, d = x.shape
    ts = min(SEQ_TILE, s // 2)
    n_pairs = s // (2 * ts)
    vec = pl.BlockSpec((1, d), lambda bi, si: (0, 0))
    return pl.pallas_call(
        functools.partial(_hgrn_kernel, dn_alpha=dn_alpha),
        grid=(b, n_pairs),
        in_specs=[
            pl.BlockSpec((1, 2 * ts, d), lambda bi, si: (bi, si, 0)),
            _next_pair_spec(b, n_pairs, ts, d),
            pl.BlockSpec((d, 4 * d), lambda bi, si: (0, 0)),
            vec, vec,
            pl.BlockSpec((d, d), lambda bi, si: (0, 0)),
            vec, vec,
        ],
        out_specs=pl.BlockSpec((1, 2 * ts, d), lambda bi, si: (bi, si, 0)),
        out_shape=jax.ShapeDtypeStruct((b, s, d), F32),
        scratch_shapes=[
            pltpu.VMEM((ts, 4 * d), F32),
            pltpu.VMEM((ts, 4 * d), F32),
            pltpu.VMEM((ts, d), BF16),
            pltpu.VMEM((ts, d), BF16),
            pltpu.VMEM((ts, d), F32),
            pltpu.VMEM((ts, d), BF16),
            pltpu.VMEM((d // HGRN_HEAD, HGRN_HEAD, HGRN_HEAD), F32),
            pltpu.VMEM((HGRN_CHUNK, d), F32),
            pltpu.VMEM((HGRN_CHUNK, d), F32),
            pltpu.VMEM((HGRN_CHUNK, d), F32),
        ],
        compiler_params=pltpu.CompilerParams(
            dimension_semantics=("arbitrary", "arbitrary"),
            vmem_limit_bytes=V7X_VMEM_LIMIT),
        name="hgrn_mixer",
    )(x, x, w_in, lb, norm_g, w_out, ln_g, ln_b)


def _ret_kernel(x_ref, xn_ref, w_in_ref, cos_ref, sin_e_ref, sin_o_ref, w_out_ref, lg_ref, lbias_ref,
                o_ref, ha_ref, hb_ref, xa_ref, xb_ref, og_ref, st_ref, *, dn_alpha):
    ts, d = xn_ref.shape[1], xn_ref.shape[2]
    qk = d
    vd = 2 * d
    hk = qk // RET_HEADS
    hv = vd // RET_HEADS
    c = RET_CHUNK
    n_pieces = (2 * qk + 2 * vd) // PROJ_PIECE
    n_chunks = ts // c
    stages = 4
    per_slot = -(-n_pieces // (n_chunks * stages))

    @pl.when(pl.program_id(1) == 0)
    def _():
        st_ref[...] = jnp.zeros_like(st_ref)

    @pl.when((pl.program_id(0) == 0) & (pl.program_id(1) == 0))
    def _():
        ha_ref[...] = _dot(x_ref[0, 0:ts, :].astype(BF16), w_in_ref[...])

    ri = lax.broadcasted_iota(jnp.int32, (c, c), 0)
    cj = lax.broadcasted_iota(jnp.int32, (c, c), 1)
    rel = (ri - cj).astype(F32)
    idx_col = lax.broadcasted_iota(jnp.int32, (c, 1), 0).astype(F32)

    def chunk(r_base, ci, h_ref, project_some):
        rows = pl.ds(ci * c, c)
        pos = pl.ds(r_base + ci * c, c)
        project_some()
        cos = jnp.tile(cos_ref[pos, :], (1, RET_HEADS))
        sin_e = jnp.tile(sin_e_ref[pos, :], (1, RET_HEADS))
        sin_o = jnp.tile(sin_o_ref[pos, :], (1, RET_HEADS))

        def rope(t):
            nxt = pltpu.roll(t, qk - 1, 1)
            prv = pltpu.roll(t, 1, 1)
            return t * cos + nxt * sin_e + prv * sin_o

        q = rope(h_ref[rows, 0:qk]).astype(BF16)
        k = rope(h_ref[rows, qk:2 * qk]) * (hk ** -0.5)
        v = h_ref[rows, 2 * qk:2 * qk + vd].astype(BF16)
        lgam = [math.log(1.0 - 2.0 ** (-5.0 - hd)) for hd in range(RET_HEADS)]
        project_some()
        staged = []
        for hd in range(RET_HEADS):
            zeta = jnp.exp((c - 1.0 - idx_col) * lgam[hd])
            q_h = q[:, hk * hd:hk * (hd + 1)]
            k_h = k[:, hk * hd:hk * (hd + 1)]
            v_h = v[:, hv * hd:hv * (hd + 1)]
            st = st_ref[hd]
            sc = _dot(q_h, k_h.astype(BF16), _NT)
            o_inter = _dot(q_h, st.astype(BF16))
            new_st = st * math.exp(c * lgam[hd]) + _dot((k_h * zeta).astype(BF16), v_h, _TN)
            staged.append((sc, o_inter, new_st, v_h))
            if hd == RET_HEADS // 2 - 1:
                project_some()
        project_some()
        for hd, (sc, o_inter, new_st, v_h) in enumerate(staged):
            decay = jnp.where(rel >= 0, jnp.exp(rel * lgam[hd]), 0.0)
            xi = jnp.exp((idx_col + 1.0) * lgam[hd])
            o_h = _dot((sc * decay).astype(BF16), v_h) + o_inter * xi
            st_ref[hd] = new_st
            mu = jnp.mean(o_h, axis=-1, keepdims=True)
            dlt = o_h - mu
            var = jnp.mean(dlt * dlt, axis=-1, keepdims=True)
            g_h = h_ref[rows, 2 * qk + vd + hv * hd:2 * qk + vd + hv * (hd + 1)]
            og_ref[rows, hv * hd:hv * (hd + 1)] = (
                (g_h * jax.nn.sigmoid(g_h)) * (dlt * lax.rsqrt(var + LN_EPS))).astype(BF16)

    def tile_pass(r_base, h_ref, h_next, x_next_ref, x_next):
        x_next_ref[...] = x_next.astype(BF16)
        pieces = iter(range(n_pieces))

        def project(piece):
            cols = slice(PROJ_PIECE * piece, PROJ_PIECE * (piece + 1))
            h_next[:, cols] = _dot(x_next_ref[...], w_in_ref[:, cols])

        def project_some():
            for _ in range(per_slot):
                piece = next(pieces, None)
                if piece is not None:
                    project(piece)

        for ci in range(n_chunks):
            chunk(r_base, ci, h_ref, project_some)
        for piece in pieces:
            project(piece)
        mix = _dot(og_ref[...], w_out_ref[...])
        o_ref[0, r_base:r_base + ts, :] = _layer_norm(
            dn_alpha * x_ref[0, r_base:r_base + ts, :] + mix, lg_ref[...], lbias_ref[...])

    tile_pass(0, ha_ref, hb_ref, xa_ref, x_ref[0, ts:2 * ts, :])
    tile_pass(ts, hb_ref, ha_ref, xb_ref, xn_ref[0])


def _ret_layer(x, w_in, cos, sin_e, sin_o, w_out, ln_g, ln_b, dn_alpha):
    b, s, d = x.shape
    ts = min(SEQ_TILE, s // 2)
    n_pairs = s // (2 * ts)
    n_in = w_in.shape[1]
    hk = d // RET_HEADS
    vec = pl.BlockSpec((1, d), lambda bi, si: (0, 0))
    tab = pl.BlockSpec((2 * ts, hk), lambda bi, si: (si, 0))
    once = pl.Buffered(1)
    return pl.pallas_call(
        functools.partial(_ret_kernel, dn_alpha=dn_alpha),
        grid=(b, n_pairs),
        in_specs=[
            pl.BlockSpec((1, 2 * ts, d), lambda bi, si: (bi, si, 0)),
            _next_pair_spec(b, n_pairs, ts, d),
            pl.BlockSpec((d, n_in), lambda bi, si: (0, 0), pipeline_mode=once),
            tab, tab, tab,
            pl.BlockSpec((2 * d, d), lambda bi, si: (0, 0), pipeline_mode=once),
            vec, vec,
        ],
        out_specs=pl.BlockSpec((1, 2 * ts, d), lambda bi, si: (bi, si, 0)),
        out_shape=jax.ShapeDtypeStruct((b, s, d), F32),
        scratch_shapes=[
            pltpu.VMEM((ts, n_in), F32),
            pltpu.VMEM((ts, n_in), F32),
            pltpu.VMEM((ts, d), BF16),
            pltpu.VMEM((ts, d), BF16),
            pltpu.VMEM((ts, 2 * d), BF16),
            pltpu.VMEM((RET_HEADS, hk, 2 * d // RET_HEADS), F32),
        ],
        compiler_params=pltpu.CompilerParams(
            dimension_semantics=("arbitrary", "arbitrary"),
            vmem_limit_bytes=V7X_VMEM_LIMIT),
        name="ret_mixer",
    )(x, x, w_in, cos, sin_e, sin_o, w_out, ln_g, ln_b)


def _rotary_tables(seq_len, hk):
    pos = jnp.arange(seq_len, dtype=F32)
    inv = 1.0 / (ROPE_BASE ** jnp.linspace(0.0, 1.0, hk // 2, dtype=F32))
    theta = pos[:, None] * jnp.repeat(inv, 2)[None, :]
    cos, sin = jnp.cos(theta), jnp.sin(theta)
    even = (jnp.arange(hk) % 2 == 0)[None, :]
    return cos, jnp.where(even, -sin, 0.0), jnp.where(even, 0.0, sin)


def _router_kernel(x_ref, w_ref, b_ref, idx_ref, gate_ref, rank_ref, cnt_ref):
    t = x_ref.shape[0]
    e = w_ref.shape[0]

    xh, xm, _ = _split3(x_ref[...])
    wh, wm, _ = _split3(w_ref[...])
    logits = (_dot(wm, xh, _NT) + _dot(wh, xm, _NT) + _dot(wh, xh, _NT)) + b_ref[...]

    eid = lax.broadcasted_iota(jnp.int32, (e, t), 0)
    ti = lax.broadcasted_iota(jnp.int32, (t, t), 0)
    tj = lax.broadcasted_iota(jnp.int32, (t, t), 1)
    before = jnp.where(ti < tj, 1.0, 0.0).astype(BF16)
    work = logits
    vals, idxs, ranks = [], [], []
    base = jnp.zeros((e, 1), F32)
    for _ in range(TOP_K):
        m = jnp.max(work, axis=0, keepdims=True)
        sel = jnp.min(jnp.where(work == m, eid, e), axis=0, keepdims=True)
        hit = eid == sel
        work = jnp.where(hit, -jnp.inf, work)
        onehot = jnp.where(hit, 1.0, 0.0)
        excl = _dot(onehot.astype(BF16), before)
        rank = jnp.sum(onehot * (excl + base), axis=0, keepdims=True)
        base = base + jnp.sum(onehot, axis=1, keepdims=True)
        vals.append(m)
        idxs.append(sel)
        ranks.append(rank)
    cnt_ref[0] = base.astype(jnp.int32)
    ex = [jnp.exp(vv - vals[0]) for vv in vals]
    den = ex[0] + ex[1] + ex[2] + ex[3]
    gate_ref[...] = jnp.concatenate(ex, axis=0) / den
    idx_ref[...] = jnp.concatenate(idxs, axis=0)
    rank_ref[...] = jnp.concatenate(ranks, axis=0).astype(jnp.int32)


def _router(x2, w_t, b_col):
    n, d = x2.shape
    e = w_t.shape[0]
    t = min(ROUTER_TILE, n)
    kt = pl.BlockSpec((TOP_K, t), lambda i: (0, i))
    return pl.pallas_call(
        _router_kernel,
        grid=(n // t,),
        in_specs=[
            pl.BlockSpec((t, d), lambda i: (i, 0)),
            pl.BlockSpec((e, d), lambda i: (0, 0)),
            pl.BlockSpec((e, 1), lambda i: (0, 0)),
        ],
        out_specs=[kt, kt, kt, pl.BlockSpec((1, e, 1), lambda i: (i, 0, 0))],
        out_shape=[
            jax.ShapeDtypeStruct((TOP_K, n), jnp.int32),
            jax.ShapeDtypeStruct((TOP_K, n), F32),
            jax.ShapeDtypeStruct((TOP_K, n), jnp.int32),
            jax.ShapeDtypeStruct((n // t, e, 1), jnp.int32),
        ],
        compiler_params=pltpu.CompilerParams(dimension_semantics=("arbitrary",)),
        name="moe_router",
    )(x2, w_t, b_col)


def _plan_kernel(cnt_ref, idx_ref, rank_ref, lpos_ref, tab_ref, fill_ref, acc_ref):
    i = pl.program_id(0)
    nt, e, _ = cnt_ref.shape
    t = idx_ref.shape[1]
    g_shift = ROW_GRANULE.bit_length() - 1
    b_shift = (EXPERT_ROWS // ROW_GRANULE).bit_length() - 1

    def units(cnt):
        return (cnt + (ROW_GRANULE - 1)) >> g_shift

    @pl.when(i == 0)
    def _():
        acc_ref[...] = jnp.zeros_like(acc_ref)

    total = units(cnt_ref[0])
    for j in range(1, nt):
        total = total + units(cnt_ref[j])
    pad_col = (((total + ((1 << b_shift) - 1)) >> b_shift) << b_shift).astype(F32)
    r = lax.broadcasted_iota(jnp.int32, (e, e), 0)
    c = lax.broadcasted_iota(jnp.int32, (e, e), 1)

    def to_row(col):
        return jnp.sum(jnp.where(r == c, col, 0.0), axis=0, keepdims=True)

    def prefix(col):
        return jnp.sum(jnp.where(c < r, to_row(col), 0.0), axis=1, keepdims=True)

    start_col = prefix(pad_col)
    end_col = start_col + pad_col
    used_col = start_col + total.astype(F32)

    mine = units(cnt_ref[i]).astype(F32)
    local_col = prefix(mine)
    global_col = start_col + acc_ref[...]
    acc_ref[...] = acc_ref[...] + mine
    tab_ref[0, 0:1, :] = to_row(local_col).astype(jnp.int32)
    tab_ref[0, 1:2, :] = to_row(mine).astype(jnp.int32)
    tab_ref[0, 2:3, :] = to_row(global_col).astype(jnp.int32)

    lane =lax.broadcasted_iota(jnp.int32, (e, fill_ref.shape[1]), 1)
    sub = lax.broadcasted_iota(jnp.int32, (e, fill_ref.shape[1]), 0)
    fill = jnp.where(lane == sub, used_col, 0.0) + jnp.where(lane == sub + e, end_col, 0.0)
    fill_ref[...] = jnp.sum(fill, axis=0, keepdims=True).astype(jnp.int32) * ROW_GRANULE

    eid = lax.broadcasted_iota(jnp.int32, (e, t), 0)
    for kk in range(TOP_K):
        hit = eid == idx_ref[kk:kk + 1, :]
        rank = rank_ref[kk:kk + 1, :]
        l_k = jnp.sum(jnp.where(hit, local_col, 0.0), axis=0, keepdims=True).astype(jnp.int32)
        lpos_ref[kk:kk + 1, :] = l_k * ROW_GRANULE + rank


def _plan(counts, idx, rank):
    nt, e, _ = counts.shape
    n = idx.shape[1]
    t = n // nt
    kt = pl.BlockSpec((TOP_K, t), lambda i: (0, i))
    fill_lanes = V7X_LANES * (-(-2 * e // V7X_LANES))
    return pl.pallas_call(
        _plan_kernel,
        grid=(nt,),
        in_specs=[pl.BlockSpec((nt, e, 1), lambda i: (0, 0, 0)), kt, kt],
        out_specs=[
            kt,
            pl.BlockSpec((1, 3, e), lambda i: (i, 0, 0)),
            pl.BlockSpec((1, fill_lanes), lambda i: (0, 0)),
        ],
        out_shape=[
            jax.ShapeDtypeStruct((TOP_K, n), jnp.int32),
            jax.ShapeDtypeStruct((nt, 3, e), jnp.int32),
            jax.ShapeDtypeStruct((1, fill_lanes), jnp.int32),
        ],
        scratch_shapes=[pltpu.VMEM((e, 1), F32)],
        compiler_params=pltpu.CompilerParams(dimension_semantics=("arbitrary",)),
        name="moe_plan",
    )(counts, idx, rank)


def _slab_copies(fn, tab_ref, step, n_experts, local_ref, slot, sorted_ref, sem, to_sorted):
    def per_expert(ee, carry):
        base = step * (3 * n_experts)
        src_u = tab_ref[base + ee]
        n_u = tab_ref[base + n_experts + ee]
        dst_u = tab_ref[base + 2 * n_experts + ee]
        for b in range(SLAB_BITS):
            @pl.when(((n_u >> b) & 1) == 1)
            def _():
                off = n_u & ((1 << b) - 1)
                rows = ROW_GRANULE << b
                loc = local_ref.at[slot, pl.ds(pl.multiple_of((src_u + off) * ROW_GRANULE, ROW_GRANULE), rows)]
                glob = sorted_ref.at[pl.ds(pl.multiple_of((dst_u + off) * ROW_GRANULE, ROW_GRANULE), rows)]
                fn(pltpu.make_async_copy(loc, glob, sem.at[slot]) if to_sorted
                   else pltpu.make_async_copy(glob, loc, sem.at[slot]))
        return carry

    lax.fori_loop(0, n_experts, per_expert, 0)


def _dispatch_kernel(tab_ref, fill_ref, x_ref, lpos_ref, xs_out, xloc_ref, zero_ref, sem_run, sem_zero,
                     *, n_experts):
    t = x_ref.shape[0]
    n_rows = xs_out.shape[0]
    local_rows = xloc_ref.shape[1]
    i = pl.program_id(0)
    slot = i % 2

    @pl.when(i == 0)
    def _():
        zero_ref[...] = jnp.zeros_like(zero_ref)
        tail_blk = fill_ref[2 * n_experts - 1] // EXPERT_ROWS

        def fill_copies(fn):
            def per_expert(ee, carry):
                def one(uu, c2):
                    r0 = pl.multiple_of(uu * ROW_GRANULE, ROW_GRANULE)
                    fn(pltpu.make_async_copy(zero_ref.at[pl.ds(0, ROW_GRANULE)],
                                             xs_out.at[pl.ds(r0, ROW_GRANULE)], sem_zero))
                    return c2
                lax.fori_loop(fill_ref[ee] // ROW_GRANULE, fill_ref[n_experts + ee] // ROW_GRANULE, one, 0)
                return carry
            lax.fori_loop(0, n_experts, per_expert, 0)

            def per_tail(bi, carry):
                r0 = pl.multiple_of(bi * EXPERT_ROWS, EXPERT_ROWS)
                fn(pltpu.make_async_copy(zero_ref, xs_out.at[pl.ds(r0, EXPERT_ROWS)], sem_zero))
                return carry
            lax.fori_loop(tail_blk, n_rows // EXPERT_ROWS, per_tail, 0)

        fill_copies(lambda cp: cp.start())
        fill_copies(lambda cp: cp.wait())

    xb = x_ref[...].astype(BF16)
    lp = lpos_ref[...]
    for rc in range(local_rows // SORT_CHUNK):
        rows = lax.broadcasted_iota(jnp.int32, (SORT_CHUNK, t), 0) + rc * SORT_CHUNK
        sel = jnp.zeros((SORT_CHUNK, t), F32)
        for kk in range(TOP_K):
            sel = jnp.where(rows == lp[kk:kk + 1, :], 1.0, sel)
        xloc_ref[slot, SORT_CHUNK * rc:SORT_CHUNK * (rc + 1), :] = _dot(sel.astype(BF16), xb)

    move = functools.partial(_slab_copies, tab_ref=tab_ref, n_experts=n_experts, local_ref=xloc_ref,
                             sorted_ref=xs_out, sem=sem_run, to_sorted=True)
    move(lambda cp: cp.start(), step=i, slot=slot)

    @pl.when(i > 0)
    def _():
        move(lambda cp: cp.wait(), step=i - 1, slot=1 - slot)

    @pl.when(i == pl.num_programs(0) - 1)
    def _():
        move(lambda cp: cp.wait(), step=i, slot=slot)


def _dispatch(x2, lpos, tab, fill, n_rows, n_experts):
    n, d = x2.shape
    nt = tab.shape[0] // (3 * n_experts)
    t = n // nt
    local_rows = -(-(TOP_K * t + (ROW_GRANULE - 1) * n_experts) // SORT_CHUNK) * SORT_CHUNK
    grid_spec = pltpu.PrefetchScalarGridSpec(
        num_scalar_prefetch=2,
        grid=(nt,),
        in_specs=[
            pl.BlockSpec((t, d), lambda i, tb, fl: (i, 0)),
            pl.BlockSpec((TOP_K, t), lambda i, tb, fl: (0, i)),
        ],
        out_specs=pl.BlockSpec(memory_space=pl.ANY),
        scratch_shapes=[
            pltpu.VMEM((2, local_rows, d), F32),
            pltpu.VMEM((EXPERT_ROWS, d), F32),
            pltpu.SemaphoreType.DMA((2,)),
            pltpu.SemaphoreType.DMA,
        ],
    )
    return pl.pallas_call(
        functools.partial(_dispatch_kernel, n_experts=n_experts),
        grid_spec=grid_spec,
        out_shape=jax.ShapeDtypeStruct((n_rows, d), F32),
        compiler_params=pltpu.CompilerParams(
            dimension_semantics=("arbitrary",), vmem_limit_bytes=V7X_VMEM_LIMIT),
        name="moe_dispatch",
    )(tab, fill, x2, lpos)


def _expert_kernel(fill_ref, xs_hbm, wgu_ref, bg_ref, bl_ref, wd_ref, bd_ref, y_hbm,
                   xbuf, ybuf, wt_ref, wg_s, wl_s, wd_s, sem_in, sem_out, sem_zero, *, n_experts):
    e = pl.program_id(0)
    n_rows = y_hbm.shape[0]
    ff, d = wg_s.shape
    tc, rk = wt_ref.shape
    half = tc // 2
    run_start = jnp.where(e == 0, 0, fill_ref[n_experts + jnp.maximum(e - 1, 0)])
    run_end = fill_ref[n_experts + e]
    n_blk = (run_end - run_start) // EXPERT_ROWS

    def rows(j):
        return pl.ds(pl.multiple_of(run_start + j * EXPERT_ROWS, EXPERT_ROWS), EXPERT_ROWS)

    def load(j):
        return pltpu.make_async_copy(xs_hbm.at[rows(j)], xbuf.at[j % 2], sem_in.at[j % 2])

    def store(j):
        return pltpu.make_async_copy(ybuf.at[j % 2], y_hbm.at[rows(j)], sem_out.at[j % 2])

    @pl.when(n_blk > 0)
    def _():
        load(0).start()
        for kb in range(d // rk):
            for j in range(2 * ff // tc):
                wt_ref[...] = wgu_ref[0, 0, rk * kb:rk * (kb + 1), tc * j:tc * (j + 1)].T
                wg_s[half * j:half * (j + 1), rk * kb:rk * (kb + 1)] = (
                    wt_ref[pl.ds(0, half, stride=2), :].astype(BF16))
                wl_s[half * j:half * (j + 1), rk * kb:rk * (kb + 1)] = (
                    wt_ref[pl.ds(1, half, stride=2), :].astype(BF16))
        wd_s[...] = wd_ref[0, 0].astype(BF16)

        def block(j, carry):
            slot = j % 2
            load(j).wait()

            @pl.when(j + 1 < n_blk)
            def _():
                load(j + 1).start()

            @pl.when(j >= 2)
            def _():
                store(j - 2).wait()

            xb = xbuf[slot].astype(BF16)
            hg = _dot(xb, wg_s[...], _NT) + bg_ref[0]
            hl = _dot(xb, wl_s[...], _NT) + bl_ref[0]
            glu = jnp.minimum(hg, SWIGLU_LIMIT)
            lin = jnp.clip(hl, -SWIGLU_LIMIT, SWIGLU_LIMIT)
            a = glu * jax.nn.sigmoid(SWIGLU_ALPHA * glu) * (lin + 1.0)
            ybuf[slot] = _dot(a.astype(BF16), wd_s[...]) + bd_ref[0, 0]
            store(j).start()
            return carry

        lax.fori_loop(0, n_blk, block, 0)

        @pl.when(n_blk >= 2)
        def _():
            store(n_blk - 2).wait()

        store(n_blk - 1).wait()

    @pl.when(e == n_experts - 1)
    def _():
        xbuf[0] = jnp.zeros((EXPERT_ROWS, d), F32)

        def tail(fn):
            def one(bi, carry):
                r0 = pl.multiple_of(bi * EXPERT_ROWS, EXPERT_ROWS)
                fn(pltpu.make_async_copy(xbuf.at[0], y_hbm.at[pl.ds(r0, EXPERT_ROWS)], sem_zero))
                return carry
            lax.fori_loop(run_end // EXPERT_ROWS, n_rows // EXPERT_ROWS, one, 0)

        tail(lambda cp: cp.start())
        tail(lambda cp: cp.wait())


def _experts(xs, fill, w_gu, b_glu, b_lin, w_dn, b_dn, layer):
    n_rows, d = xs.shape
    ff = w_dn.shape[2]
    e = w_dn.shape[1]
    wspec = lambda r, c: pl.BlockSpec((1, r, c), lambda i, fl: (i, 0, 0))
    lspec = lambda r, c: pl.BlockSpec((1, 1, r, c), lambda i, fl: (layer, i, 0, 0))
    grid_spec = pltpu.PrefetchScalarGridSpec(
        num_scalar_prefetch=1,
        grid=(e,),
        in_specs=[
            pl.BlockSpec(memory_space=pl.ANY),
            lspec(d, 2 * ff), wspec(1, ff), wspec(1, ff),
            lspec(ff, d), lspec(1, d),
        ],
        out_specs=pl.BlockSpec(memory_space=pl.ANY),
        scratch_shapes=[
            pltpu.VMEM((2, EXPERT_ROWS, d), F32),
            pltpu.VMEM((2, EXPERT_ROWS, d), F32),
            pltpu.VMEM((WEIGHT_SPLIT_COLS, V7X_LANES), F32),
            pltpu.VMEM((ff, d), BF16),
            pltpu.VMEM((ff, d), BF16),
            pltpu.VMEM((ff, d), BF16),
            pltpu.SemaphoreType.DMA((2,)),
            pltpu.SemaphoreType.DMA((2,)),
            pltpu.SemaphoreType.DMA,
        ],
    )
    return pl.pallas_call(
        functools.partial(_expert_kernel, n_experts=e),
        grid_spec=grid_spec,
        out_shape=jax.ShapeDtypeStruct((n_rows, d), F32),
        compiler_params=pltpu.CompilerParams(
            dimension_semantics=("arbitrary",), vmem_limit_bytes=V7X_VMEM_LIMIT),
        name="moe_experts",
    )(fill, xs, w_gu, b_glu, b_lin, w_dn, b_dn)


def _combine_kernel(tab_ref, y_hbm, x_ref, lpos_ref, gate_ref, lg_ref, lbias_ref, o_ref,
                    yloc_ref, sem_run, *, dn_alpha, n_experts):
    t = x_ref.shape[0]
    local_rows = yloc_ref.shape[1]
    i = pl.program_id(0)
    slot = i % 2
    fetch = functools.partial(_slab_copies, tab_ref=tab_ref, n_experts=n_experts, local_ref=yloc_ref,
                              sorted_ref=y_hbm, sem=sem_run, to_sorted=False)

    @pl.when(i == 0)
    def _():
        yloc_ref[...] = jnp.zeros_like(yloc_ref)
        fetch(lambda cp: cp.start(), step=0, slot=0)

    @pl.when(i + 1 < pl.num_programs(0))
    def _():
        fetch(lambda cp: cp.start(), step=i + 1, slot=1 - slot)

    fetch(lambda cp: cp.wait(), step=i, slot=slot)

    lp = lpos_ref[...]
    gate = gate_ref[...]
    ffn = jnp.zeros(x_ref.shape, F32)
    for rc in range(local_rows // SORT_CHUNK):
        cols = lax.broadcasted_iota(jnp.int32, (t, SORT_CHUNK), 1) + rc * SORT_CHUNK
        w = jnp.zeros((t, SORT_CHUNK), F32)
        for kk in range(TOP_K):
            w = jnp.where(cols == lp[:, kk:kk + 1], gate[:, kk:kk + 1], w)
        y_c = yloc_ref[slot, SORT_CHUNK * rc:SORT_CHUNK * (rc + 1), :].astype(BF16)
        ffn = ffn + _dot(w.astype(BF16), y_c)
    o_ref[...] = _layer_norm(dn_alpha * x_ref[...] + ffn, lg_ref[...], lbias_ref[...])


def _combine(y, x2, lpos_t, gate_t, tab, ln_g, ln_b, dn_alpha, n_experts):
    n, d = x2.shape
    nt = tab.shape[0] // (3 * n_experts)
    t = n // nt
    local_rows = -(-(TOP_K * t + (ROW_GRANULE - 1) * n_experts) // SORT_CHUNK) * SORT_CHUNK
    vec = pl.BlockSpec((1, d), lambda i, tb: (0, 0))
    tk = pl.BlockSpec((t, TOP_K), lambda i, tb: (i, 0))
    grid_spec = pltpu.PrefetchScalarGridSpec(
        num_scalar_prefetch=1,
        grid=(nt,),
        in_specs=[
            pl.BlockSpec(memory_space=pl.ANY),
            pl.BlockSpec((t, d), lambda i, tb: (i, 0)),
            tk, tk, vec, vec,
        ],
        out_specs=pl.BlockSpec((t, d), lambda i, tb: (i, 0)),
        scratch_shapes=[
            pltpu.VMEM((2, local_rows, d), F32),
            pltpu.SemaphoreType.DMA((2,)),
        ],
    )
    return pl.pallas_call(
        functools.partial(_combine_kernel, dn_alpha=dn_alpha, n_experts=n_experts),
        grid_spec=grid_spec,
        out_shape=jax.ShapeDtypeStruct((n, d), F32),
        compiler_params=pltpu.CompilerParams(
            dimension_semantics=("arbitrary",), vmem_limit_bytes=V7X_VMEM_LIMIT),
        name="moe_combine",
    )(tab, y, x2, lpos_t, gate_t, ln_g, ln_b)


def _moe_layer(x2, w_r, b_r, w_gu, b_glu, b_lin, w_dn, b_dn, layer, ln_g, ln_b, dn_alpha):
    n, d = x2.shape
    e = w_r.shape[1]
    idx, gate, rank, counts = _router(x2, w_r.T, b_r.reshape(e, 1))
    n_blocks = -(-(n * TOP_K + (ROW_GRANULE - 1) * counts.shape[0] * e) // EXPERT_ROWS) + e
    lpos, tab, fill = _plan(counts, idx, rank)
    fill = fill.reshape(-1)
    tab = tab.reshape(-1)
    xs = _dispatch(x2, lpos, tab, fill, n_blocks * EXPERT_ROWS, e)
    y = _experts(xs, fill, w_gu, b_glu, b_lin, w_dn, b_dn, layer)
    return _combine(y, x2, lpos.T, gate.T, tab, ln_g, ln_b, dn_alpha, e)


def kernel(x, hgrn_w_in, hgrn_norm_g, hgrn_w_out, lower_bounds, ret_w_in, ret_w_out, ln_mix_g, ln_mix_b,
           ln_ffn_g, ln_ffn_b, router_w, router_b, expert_w_gu, expert_b_gu, expert_w_down, expert_b_down):
    b, s, d = x.shape
    depth = lower_bounds.shape[0]
    dn_alpha = (2 * depth) ** 0.25
    lb_soft = jax.nn.softmax(lower_bounds.astype(F32), axis=0)
    lbs = jnp.cumsum(lb_soft, axis=0) - lb_soft[0]
    cos, sin_e, sin_o = _rotary_tables(s, d // RET_HEADS)
    row = lambda p, i: p[i].reshape(1, -1)
    for i in range(depth):
        j = i // 2
        if i % 2 == 0:
            x = _hgrn_layer(x, hgrn_w_in[j].astype(BF16), row(lbs, i), row(hgrn_norm_g, j),
                            hgrn_w_out[j].astype(BF16), row(ln_mix_g, i), row(ln_mix_b, i), dn_alpha)
        else:
            x = _ret_layer(x, ret_w_in[j].astype(BF16), cos, sin_e, sin_o, ret_w_out[j].astype(BF16),
                           row(ln_mix_g, i), row(ln_mix_b, i), dn_alpha)
        b_gu = expert_b_gu[i]
        x2 = _moe_layer(
            x.reshape(b * s, d), router_w[i], router_b[i], expert_w_gu,
            b_gu[:, None, 0::2], b_gu[:, None, 1::2],
            expert_w_down, expert_b_down[:, :, None, :], i,
            row(ln_ffn_g, i), row(ln_ffn_b, i), dn_alpha)
        x = x2.reshape(b, s, d)
    return x
```
